```python
import functools
import jax, jax.numpy as jnp
from jax import lax
import numpy as np

D_MODEL = 1024
BATCH = 4
SEQ = 8192
DEPTH = 1
DEC_BATCH = 32
DEC_SEQ = 1
PAST_LEN = 16384
PAGE_SIZE = 128

HEAD_DIM = 64
D_ATTN = D_MODEL // 2
N_HEADS = D_ATTN // HEAD_DIM
D_POOL = D_MODEL // 2
POOL_WINDOWS = (2, 4, 8, 16)
POOL_GROUPS = len(POOL_WINDOWS)
POOL_CH = D_POOL // POOL_GROUPS
POOL_BUF = max(POOL_WINDOWS) - 1
N_BRANCH = 2
MOBA_BLOCK = 256
MOBA_TOPK = 3
Q_BLOCK = 64
D_FF = 4 * D_MODEL
D_PLE = 256
D_IN = 3 * D_ATTN + D_POOL + N_BRANCH * D_MODEL
ALPHA = (2.0 * DEPTH) ** 0.25
BETA = (8.0 * DEPTH) ** -0.25
LN_EPS = 1e-5
NEG_INF = -1e30

kernel_name = 'moba_pool_gated_hybrid_step'


def alibi_slopes():
    return jnp.asarray(np.array([2.0 ** (-8.0 * (h + 1) / N_HEADS) for h in range(N_HEADS)], dtype=np.float32))


def layer_norm(x, g, b):
    xf = x.astype(jnp.float32)
    mu = xf.mean(-1, keepdims=True)
    var = jnp.square(xf - mu).mean(-1, keepdims=True)
    return ((xf - mu) * lax.rsqrt(var + LN_EPS) * g + b).astype(x.dtype)


def prep_kv(k_all, v_all):
    B, L, H, Dh = k_all.shape
    nb = max(-(-L // MOBA_BLOCK) + 1, MOBA_TOPK)
    pad = ((0, 0), (0, nb * MOBA_BLOCK - L), (0, 0), (0, 0))
    kpad = jnp.pad(k_all, pad)
    vpad = jnp.pad(v_all, pad)
    kb = kpad.reshape(B, nb, MOBA_BLOCK, H, Dh)
    vb = vpad.reshape(B, nb, MOBA_BLOCK, H, Dh)
    kmean = kb.astype(jnp.float32).mean(axis=2)
    return kpad, vpad, kb, vb, kmean


def moba_block(q, q0, kpad, vpad, kb, vb, kmean):
    B, Q, H, Dh = q.shape
    NB = kb.shape[1]
    scale = Dh ** -0.5
    slopes = alibi_slopes()
    t = q0 + jnp.arange(Q, dtype=jnp.int32)
    t_blk = t // MOBA_BLOCK
    gate = jnp.einsum('bqhd,bnhd->bhqn', q, kmean, preferred_element_type=jnp.float32)
    past = jnp.arange(NB, dtype=jnp.int32)[None, :] < t_blk[:, None]
    gate = jnp.where(past[None, None], gate, NEG_INF)
    _, idx = lax.top_k(gate, MOBA_TOPK)
    slot_ok = jnp.arange(MOBA_TOPK, dtype=jnp.int32)[None, :] < jnp.minimum(t_blk, MOBA_TOPK)[:, None]
    bi = jnp.arange(B)[:, None, None, None]
    hi = jnp.arange(H)[None, :, None, None]
    k_sel = kb[bi, idx, :, hi]
    v_sel = vb[bi, idx, :, hi]
    pos_sel = idx[..., None] * MOBA_BLOCK + jnp.arange(MOBA_BLOCK, dtype=jnp.int32)
    dist_sel = (t[None, None, :, None, None] - pos_sel).astype(jnp.float32)
    s_sel = (jnp.einsum('bqhd,bhqjkd->bhqjk', q, k_sel, preferred_element_type=jnp.float32) * scale
             - slopes[None, :, None, None, None] * dist_sel)
    s_sel = jnp.where(slot_ok[None, None, :, :, None], s_sel, NEG_INF).reshape(B, H, Q, MOBA_TOPK * MOBA_BLOCK)
    start = (q0 // MOBA_BLOCK) * MOBA_BLOCK
    band = MOBA_BLOCK + Q
    k_band = lax.dynamic_slice_in_dim(kpad, start, band, axis=1)
    v_band = lax.dynamic_slice_in_dim(vpad, start, band, axis=1)
    pos_band = start + jnp.arange(band, dtype=jnp.int32)
    dist_band = (t[:, None] - pos_band[None, :]).astype(jnp.float32)
    own = ((pos_band[None, :] // MOBA_BLOCK) == t_blk[:, None]) & (dist_band >= 0)
    s_band = (jnp.einsum('bqhd,bkhd->bhqk', q, k_band, preferred_element_type=jnp.float32) * scale
              - slopes[None, :, None, None] * dist_band[None, None])
    s_band = jnp.where(own[None, None], s_band, NEG_INF)
    probs = jax.nn.softmax(jnp.concatenate([s_sel, s_band], axis=-1), axis=-1)
    n_sel = MOBA_TOPK * MOBA_BLOCK
    p_sel = probs[..., :n_sel].reshape(B, H, Q, MOBA_TOPK, MOBA_BLOCK)
    p_band = probs[..., n_sel:]
    out = (jnp.einsum('bhqjk,bhqjkd->bqhd', p_sel, v_sel.astype(jnp.float32))
           + jnp.einsum('bhqk,bkhd->bqhd', p_band, v_band.astype(jnp.float32)))
    return out.astype(q.dtype)


def attend_prompt(q, k, v):
    B, S, H, Dh = q.shape
    kpad, vpad, kb, vb, kmean = prep_kv(k, v)
    nqb = S // Q_BLOCK
    qb = q.reshape(B, nqb, Q_BLOCK, H, Dh).transpose(1, 0, 2, 3, 4)
    out = lax.map(lambda a: moba_block(a[0], a[1] * Q_BLOCK, kpad, vpad, kb, vb, kmean),
                  (qb, jnp.arange(nqb, dtype=jnp.int32)))
    return out.transpose(1, 0, 2, 3, 4).reshape(B, S, H, Dh)


def attend_sample(q, k, v, cache_k, cache_v, page_table):
    Bd, T, H, Dh = q.shape
    past_len = page_table.shape[1] * cache_k.shape[1]
    past_k = cache_k[page_table].reshape(Bd, past_len, H, Dh).astype(k.dtype)
    past_v = cache_v[page_table].reshape(Bd, past_len, H, Dh).astype(v.dtype)
    k_all = jnp.concatenate([past_k, k], axis=1)
    v_all = jnp.concatenate([past_v, v], axis=1)
    kpad, vpad, kb, vb, kmean = prep_kv(k_all, v_all)
    return moba_block(q, past_len, kpad, vpad, kb, vb, kmean)


def pool_mix(u_ext, t0, w_pool, pool_scale):
    B, L, _ = u_ext.shape
    T = L - POOL_BUF
    uf = u_ext.astype(jnp.float32)
    cs = jnp.concatenate([jnp.zeros((B, 1, D_POOL), jnp.float32), jnp.cumsum(uf, axis=1)], axis=1)
    hi = cs[:, POOL_BUF + 1:]
    sums = []
    for gi, w in enumerate(POOL_WINDOWS):
        ch = slice(gi * POOL_CH, (gi + 1) * POOL_CH)
        sums.append(hi[..., ch] - cs[:, POOL_BUF + 1 - w:POOL_BUF + 1 - w + T, ch])
    pooled = jnp.stack(sums, axis=2)
    pos = t0 + jnp.arange(T, dtype=jnp.int32)
    cnt = jnp.minimum(jnp.asarray(POOL_WINDOWS, jnp.int32)[None, :], pos[:, None] + 1).astype(jnp.float32)
    d = pooled / cnt[None, :, :, None] - uf[:, POOL_BUF:].reshape(B, T, POOL_GROUPS, POOL_CH)
    y = jnp.einsum('btgc,gcd->btgd', d, w_pool.astype(jnp.float32)).reshape(B, T, D_POOL) * pool_scale
    return y.astype(u_ext.dtype)


def trunk_layer(x, p, attend, u_buf, t0, w_in, w_branch, w_out, ln1_g, ln1_b, w_up, w_down,
                ln2_g, ln2_b, w_pe, w_pg, w_pool, pool_scale):
    B, T, _ = x.shape
    z = x @ w_in
    q, k, v, u, g = jnp.split(z, [D_ATTN, 2 * D_ATTN, 3 * D_ATTN, 3 * D_ATTN + D_POOL], axis=-1)
    q = q.reshape(B, T, N_HEADS, HEAD_DIM)
    k = k.reshape(B, T, N_HEADS, HEAD_DIM)
    v = v.reshape(B, T, N_HEADS, HEAD_DIM)
    y_a = attend(q, k, v).reshape(B, T, D_ATTN)
    u_ext = jnp.concatenate([u_buf.astype(u.dtype), u], axis=1)
    y_b = pool_mix(u_ext, t0, w_pool, pool_scale)
    branches = jnp.einsum('btnc,ncd->btnd', jnp.stack([y_a, y_b], axis=2), w_branch)
    gates = jax.nn.sigmoid(g.reshape(B, T, N_BRANCH, D_MODEL))
    mix = (gates * branches).sum(axis=2) @ w_out
    h1 = layer_norm(ALPHA * x + mix, ln1_g, ln1_b)
    ff = jnp.square(jax.nn.relu(h1 @ w_up)) @ w_down
    ple = jax.nn.sigmoid(h1 @ w_pg) * (p @ w_pe)
    h2 = layer_norm(ALPHA * h1 + ff + ple, ln2_g, ln2_b)
    return h2, k, v, u_ext[:, -POOL_BUF:]


def setup_inputs(seed: int = 0) -> dict:
    key = jax.random.key(seed)
    ks = jax.random.split(key, 21)
    n_pages = PAST_LEN // PAGE_SIZE
    n_used = DEC_BATCH * n_pages
    n_phys = (n_used * 5) // 4

    def nrm(k, shape, scale):
        return jax.random.normal(k, shape, jnp.float32) * scale

    return {
        'x_prompt': nrm(ks[0], (BATCH, SEQ, D_MODEL), 1.0),
        'x_sample': nrm(ks[1], (DEC_BATCH, DEC_SEQ, D_MODEL), 1.0),
        'cache_k': nrm(ks[2], (DEPTH, n_phys, PAGE_SIZE, N_HEADS, HEAD_DIM), 1.0),
        'cache_v': nrm(ks[3], (DEPTH, n_phys, PAGE_SIZE, N_HEADS, HEAD_DIM), 1.0),
        'state_pool': nrm(ks[4], (DEPTH, DEC_BATCH, POOL_BUF, D_POOL), 1.0),
        'page_table': jax.random.permutation(ks[5], n_phys)[:n_used].reshape(DEC_BATCH, n_pages).astype(jnp.int32),
        'p_prompt': nrm(ks[6], (DEPTH, BATCH, SEQ, D_PLE), 1.0),
        'p_sample': nrm(ks[7], (DEPTH, DEC_BATCH, DEC_SEQ, D_PLE), 1.0),
        'w_in': nrm(ks[8], (DEPTH, D_MODEL, D_IN), D_MODEL ** -0.5),
        'w_branch': nrm(ks[9], (DEPTH, N_BRANCH, D_ATTN, D_MODEL), D_ATTN ** -0.5),
        'w_out': nrm(ks[10], (DEPTH, D_MODEL, D_MODEL), BETA * D_MODEL ** -0.5),
        'ln1_g': 1.0 + nrm(ks[11], (DEPTH, D_MODEL), 0.05),
        'ln1_b': nrm(ks[12], (DEPTH, D_MODEL), 0.02),
        'w_up': nrm(ks[13], (DEPTH, D_MODEL, D_FF), D_MODEL ** -0.5),
        'w_down': nrm(ks[14], (DEPTH, D_FF, D_MODEL), BETA * D_FF ** -0.5),
        'ln2_g': 1.0 + nrm(ks[15], (DEPTH, D_MODEL), 0.05),
        'ln2_b': nrm(ks[16], (DEPTH, D_MODEL), 0.02),
        'w_pe': nrm(ks[17], (DEPTH, D_PLE, D_MODEL), BETA * D_PLE ** -0.5),
        'w_pg': nrm(ks[18], (DEPTH, D_MODEL, D_MODEL), D_MODEL ** -0.5),
        'w_pool': nrm(ks[19], (DEPTH, POOL_GROUPS, POOL_CH, POOL_CH), POOL_CH ** -0.5),
        'pool_scale': 0.5 + nrm(ks[20], (DEPTH, D_POOL), 0.1),
    }


def reference(x_prompt, x_sample, cache_k, cache_v, state_pool, page_table, p_prompt, p_sample,
              w_in, w_branch, w_out, ln1_g, ln1_b, w_up, w_down, ln2_g, ln2_b, w_pe, w_pg,
              w_pool, pool_scale):
    y_p = x_prompt
    y_s = x_sample
    kp_l, vp_l, up_l, ks_l, vs_l, us_l = [], [], [], [], [], []
    for i in range(DEPTH):
        lw = (w_in[i], w_branch[i], w_out[i], ln1_g[i], ln1_b[i], w_up[i], w_down[i],
              ln2_g[i], ln2_b[i], w_pe[i], w_pg[i], w_pool[i], pool_scale[i])
        zero_buf = jnp.zeros((y_p.shape[0], POOL_BUF, D_POOL), y_p.dtype)
        y_p, kp, vp, up = trunk_layer(y_p, p_prompt[i], attend_prompt, zero_buf, 0, *lw)
        att_s = functools.partial(attend_sample, cache_k=cache_k[i], cache_v=cache_v[i], page_table=page_table)
        y_s, ks_, vs_, us_ = trunk_layer(y_s, p_sample[i], att_s, state_pool[i], PAST_LEN, *lw)
        kp_l.append(kp)
        vp_l.append(vp)
        up_l.append(up)
        ks_l.append(ks_)
        vs_l.append(vs_)
        us_l.append(us_)
    k_prompt = jnp.stack(kp_l)
    v_prompt = jnp.stack(vp_l)
    pool_prompt = jnp.stack(up_l)
    k_sample = jnp.stack(ks_l)
    v_sample = jnp.stack(vs_l)
    pool_sample = jnp.stack(us_l)
    return (y_p, y_s, k_prompt, v_prompt, pool_prompt, k_sample, v_sample, pool_sample)
```

```python
import functools

import jax
import jax.numpy as jnp
from jax import lax
from jax.experimental import pallas as pl
from jax.experimental.pallas import tpu as pltpu

HEAD_DIM = 64
MOBA_BLOCK = 256
MOBA_TOPK = 3
POOL_WINDOWS = (2, 4, 8, 16)
POOL_BUF = max(POOL_WINDOWS) - 1
POOL_HALO = 16
LN_EPS = 1e-5
NEG_INF = -1e30
LANES = 128
VMEM_LIMIT_BYTES = 56 * 1024 * 1024

F32 = jnp.float32
BF16 = jnp.bfloat16


def _dot(a, b):
    return jnp.dot(a, b, preferred_element_type=F32)


def _dot_nt(a, b):
    return lax.dot_general(a, b, (((1,), (1,)), ((), ())), preferred_element_type=F32)


def _resident(block_shape, index_map):
    return pl.BlockSpec(block_shape, index_map, pipeline_mode=pl.Buffered(1))


def _params(semantics):
    return pltpu.CompilerParams(dimension_semantics=semantics, vmem_limit_bytes=VMEM_LIMIT_BYTES)


def _sigmoid(x):
    return 1.0 / (1.0 + jnp.exp(-x))


def _layer_norm(x, g, b):
    mu = jnp.mean(x, axis=-1, keepdims=True)
    xc = x - mu
    var = jnp.mean(xc * xc, axis=-1, keepdims=True)
    return xc * lax.rsqrt(var + LN_EPS) * g + b


def _proj_prompt_kernel(x_ref, w_ref, k_ref, v_ref, u_ref, sg_ref, qt_ref, ka_ref, vt_ref, km_ref,
                        *, d_attn, d_pool, n_heads):
    tm = x_ref.shape[0]
    xb = x_ref[...].astype(BF16)
    o_k, o_v, o_u, o_g = d_attn, 2 * d_attn, 3 * d_attn, 3 * d_attn + d_pool

    q = _dot(xb, w_ref[:, 0:o_k]) * (HEAD_DIM ** -0.5)
    qt = q.T
    row = lax.broadcasted_iota(jnp.int32, (HEAD_DIM, tm), 0)
    for h in range(n_heads):
        slope = 2.0 ** (-8.0 * (h + 1) / n_heads)
        qt_ref[0, h, 0:HEAD_DIM, :] = qt[h * HEAD_DIM:(h + 1) * HEAD_DIM, :].astype(BF16)
        extra = jnp.where(row == 0, 16.0 * slope, jnp.where(row == 1, slope, 0.0))
        qt_ref[0, h, HEAD_DIM:2 * HEAD_DIM, :] = extra.astype(BF16)

    k = _dot(xb, w_ref[:, o_k:o_v])
    kt = k.T
    for h in range(n_heads):
        k_ref[0, h] = kt[h * HEAD_DIM:(h + 1) * HEAD_DIM, :]
    for c in range(tm // MOBA_BLOCK):
        blk = k[c * MOBA_BLOCK:(c + 1) * MOBA_BLOCK, :]
        km_ref[c] = jnp.sum(blk, axis=0, keepdims=True) * (1.0 / MOBA_BLOCK)
    prow = lax.broadcasted_iota(jnp.int32, (tm, LANES), 0) & (MOBA_BLOCK - 1)
    lane = lax.broadcasted_iota(jnp.int32, (tm, LANES), 1)
    posf = jnp.where(lane == HEAD_DIM, (prow >> 4).astype(F32),
                     jnp.where(lane == HEAD_DIM + 1, (prow & 15).astype(F32), 0.0))
    low = lane < HEAD_DIM
    for m in range(n_heads // 2):
        col = k[:, m * LANES:(m + 1) * LANES]
        ka_ref[0, 2 * m] = jnp.where(low, col, posf).astype(BF16)
        ka_ref[0, 2 * m + 1] = jnp.where(low, pltpu.roll(col, HEAD_DIM, 1), posf).astype(BF16)

    vt = _dot(xb, w_ref[:, o_v:o_u]).T
    for h in range(n_heads):
        v_ref[0, h] = vt[h * HEAD_DIM:(h + 1) * HEAD_DIM, :]
        for c in range(tm // MOBA_BLOCK):
            vt_ref[0, h, c] = vt[h * HEAD_DIM:(h + 1) * HEAD_DIM,
                                 c * MOBA_BLOCK:(c + 1) * MOBA_BLOCK].astype(BF16)

    u_ref[...] = _dot(xb, w_ref[:, o_u:o_g])
    sg_ref[...] = _sigmoid(_dot(xb, w_ref[:, o_g:])).astype(BF16)


def _proj_prompt(x2d, w_in, batch, seq, tm=512):
    t, d_model = x2d.shape
    d_attn = d_model // 2
    d_pool = d_model // 2
    n_heads = d_attn // HEAD_DIM
    d_in = w_in.shape[1]
    n_gate = d_in - 3 * d_attn - d_pool
    nt = seq // tm
    nblk = seq // MOBA_BLOCK
    cb = tm // MOBA_BLOCK
    out_shape = (
        jax.ShapeDtypeStruct((batch, n_heads, HEAD_DIM, seq), F32),
        jax.ShapeDtypeStruct((batch, n_heads, HEAD_DIM, seq), F32),
        jax.ShapeDtypeStruct((t, d_pool), F32),
        jax.ShapeDtypeStruct((t, n_gate), BF16),
        jax.ShapeDtypeStruct((batch, n_heads, 2 * HEAD_DIM, seq), BF16),
        jax.ShapeDtypeStruct((batch, n_heads, seq, 2 * HEAD_DIM), BF16),
        jax.ShapeDtypeStruct((batch, n_heads, nblk, HEAD_DIM, MOBA_BLOCK), BF16),
        jax.ShapeDtypeStruct((t // MOBA_BLOCK, 1, d_attn), F32),
    )
    out_specs = (
        pl.BlockSpec((1, n_heads, HEAD_DIM, tm), lambda i: (i // nt, 0, 0, i % nt)),
        pl.BlockSpec((1, n_heads, HEAD_DIM, tm), lambda i: (i // nt, 0, 0, i % nt)),
        pl.BlockSpec((tm, d_pool), lambda i: (i, 0)),
        pl.BlockSpec((tm, n_gate), lambda i: (i, 0)),
        pl.BlockSpec((1, n_heads, 2 * HEAD_DIM, tm), lambda i: (i // nt, 0, 0, i % nt)),
        pl.BlockSpec((1, n_heads, tm, 2 * HEAD_DIM), lambda i: (i // nt, 0, i % nt, 0)),
        pl.BlockSpec((1, n_heads, cb, HEAD_DIM, MOBA_BLOCK), lambda i: (i // nt, 0, i % nt, 0, 0)),
        pl.BlockSpec((cb, 1, d_attn), lambda i: (i, 0, 0)),
    )
    return pl.pallas_call(
        functools.partial(_proj_prompt_kernel, d_attn=d_attn, d_pool=d_pool, n_heads=n_heads),
        out_shape=out_shape,
        grid=(t // tm,),
        in_specs=[pl.BlockSpec((tm, d_model), lambda i: (i, 0)),
                  _resident((d_model, d_in), lambda i: (0, 0))],
        out_specs=out_specs,
        compiler_params=_params(("arbitrary",)),
        name="proj_prompt",
    )(x2d, w_in)


def _attn_prompt_kernel(slope_ref, qt_ref, ka_ref, vt_ref, km_ref, o_ref, mask_ref):
    qi = pl.program_id(2)
    nrow = mask_ref.shape[0]
    tq = MOBA_BLOCK
    qt = qt_ref[0, 0]
    slope = slope_ref[0]

    gate = _dot(km_ref[0, 0], qt)
    row = lax.broadcasted_iota(jnp.int32, (nrow, tq), 0).astype(F32)
    qif = jnp.full((1, tq), qi, jnp.int32).astype(F32)
    g = jnp.where(row < qif, gate, NEG_INF)
    sel = jnp.zeros((nrow, tq), jnp.bool_)
    for r in range(MOBA_TOPK):
        mx = jnp.max(g, axis=0, keepdims=True)
        idx = jnp.min(jnp.where(g == mx, row, float(nrow)), axis=0, keepdims=True)
        hit = row == idx
        sel = jnp.logical_or(sel, jnp.logical_and(hit, qif > float(r)))
        g = jnp.where(hit, -jnp.inf, g)
    mask_ref[...] = jnp.where(sel, slope * float(MOBA_BLOCK) * (row - qif), NEG_INF)

    start = pl.multiple_of(qi * MOBA_BLOCK, MOBA_BLOCK)
    s = _dot(ka_ref[0, 0, pl.ds(start, MOBA_BLOCK), :], qt)
    ik = lax.broadcasted_iota(jnp.int32, (MOBA_BLOCK, tq), 0)
    iq = lax.broadcasted_iota(jnp.int32, (MOBA_BLOCK, tq), 1)
    s = jnp.where(ik <= iq, s, NEG_INF)
    m0 = jnp.max(s, axis=0, keepdims=True)
    p = jnp.exp(s - m0)
    l0 = jnp.sum(p, axis=0, keepdims=True)
    acc0 = _dot(vt_ref[0, 0, qi], p.astype(BF16))

    def body(j, carry):
        m, l, acc = carry
        off = pl.multiple_of(j * MOBA_BLOCK, MOBA_BLOCK)
        s = _dot(ka_ref[0, 0, pl.ds(off, MOBA_BLOCK), :], qt) + mask_ref[pl.ds(j, 1), :]
        m_new = jnp.maximum(m, jnp.max(s, axis=0, keepdims=True))
        alpha = jnp.exp(m - m_new)
        p = jnp.exp(s - m_new)
        l = alpha * l + jnp.sum(p, axis=0, keepdims=True)
        acc = alpha * acc + _dot(vt_ref[0, 0, j], p.astype(BF16))
        return m_new, l, acc

    _, l, acc = lax.fori_loop(0, qi, body, (m0, l0, acc0))
    o_ref[0, 0] = acc * (1.0 / l)


def _attn_prompt(slopes, qt, ka, vt, km):
    batch, n_heads, _, seq = qt.shape
    nblk = seq // MOBA_BLOCK
    nrow = km.shape[2]
    return pl.pallas_call(
        _attn_prompt_kernel,
        out_shape=jax.ShapeDtypeStruct((batch, n_heads, HEAD_DIM, seq), F32),
        grid=(batch, n_heads, nblk),
        in_specs=[
            pl.BlockSpec((1, 1, MOBA_BLOCK), lambda b, h, i: (h, 0, 0)),
            pl.BlockSpec((1, 1, 2 * HEAD_DIM, MOBA_BLOCK), lambda b, h, i: (b, h, 0, i)),
            pl.BlockSpec((1, 1, seq, 2 * HEAD_DIM), lambda b, h, i: (b, h, 0, 0)),
            pl.BlockSpec((1, 1, nblk, HEAD_DIM, MOBA_BLOCK), lambda b, h, i: (b, h, 0, 0, 0)),
            pl.BlockSpec((1, 1, nrow, 2 * HEAD_DIM), lambda b, h, i: (b, h, 0, 0)),
        ],
        out_specs=pl.BlockSpec((1, 1, HEAD_DIM, MOBA_BLOCK), lambda b, h, i: (b, h, 0, i)),
        scratch_shapes=[pltpu.VMEM((nrow, MOBA_BLOCK), F32)],
        compiler_params=_params(("arbitrary", "arbitrary", "arbitrary")),
        name="attn_prompt",
    )(slopes, qt, ka, vt, km)


def _tail(x, ya_b, d, sg_ref, p_ref, wp_ref, ps_ref, wb_ref, wo_ref, g1_ref, b1_ref, wu_ref, wd_ref,
          g2_ref, b2_ref, wpe_ref, wpg_ref, o_ref, yb_ref, *, alpha, ff_chunk):
    d_model = x.shape[1]
    pool_ch = wp_ref.shape[1]
    for g in range(len(POOL_WINDOWS)):
        sl = slice(g * pool_ch, (g + 1) * pool_ch)
        yb = _dot(d[g].astype(BF16), wp_ref[g]) * ps_ref[:, sl]
        yb_ref[:, sl] = yb.astype(BF16)
    br0 = _dot(ya_b, wb_ref[0])
    br1 = _dot(yb_ref[...], wb_ref[1])
    merged = sg_ref[:, 0:d_model].astype(F32) * br0 + sg_ref[:, d_model:].astype(F32) * br1
    mix = _dot(merged.astype(BF16), wo_ref[...])
    h1 = _layer_norm(alpha * x + mix, g1_ref[...], b1_ref[...])
    h1b = h1.astype(BF16)
    ple = _sigmoid(_dot(h1b, wpg_ref[...])) * _dot(p_ref[...].astype(BF16), wpe_ref[...])
    acc = alpha * h1 + ple
    d_ff = wu_ref.shape[1]
    for c in range(d_ff // ff_chunk):
        a = jnp.maximum(_dot(h1b, wu_ref[:, c * ff_chunk:(c + 1) * ff_chunk]), 0.0)
        acc = acc + _dot((a * a).astype(BF16), wd_ref[c * ff_chunk:(c + 1) * ff_chunk, :])
    o_ref[...] = _layer_norm(acc, g2_ref[...], b2_ref[...])


def _tail_prompt_kernel(x_ref, yt_ref, u_ref, halo_ref, sg_ref, p_ref, *rest, tiles_per_seq, alpha, ff_chunk):
    *w_refs, o_ref, yb_ref, ue_ref = rest
    tm = x_ref.shape[0]
    ti = pl.program_id(0) % tiles_per_seq
    u = u_ref[...]
    ue_ref[POOL_HALO:POOL_HALO + tm, :] = u

    @pl.when(ti == 0)
    def _():
        ue_ref[0:POOL_HALO, :] = jnp.zeros((POOL_HALO, u.shape[1]), F32)

    @pl.when(ti != 0)
    def _():
        ue_ref[0:POOL_HALO, :] = halo_ref[...]

    pool_ch = u.shape[1] // len(POOL_WINDOWS)
    pos = ti * tm + lax.broadcasted_iota(jnp.int32, (tm, pool_ch), 0)
    d = []
    for g, w in enumerate(POOL_WINDOWS):
        sl = slice(g * pool_ch, (g + 1) * pool_ch)
        pooled = u[:, sl]
        for s in range(1, w):
            pooled = pooled + ue_ref[POOL_HALO - s:POOL_HALO - s + tm, sl]
        cnt = jnp.minimum(w, pos + 1).astype(F32)
        d.append(pooled / cnt - u[:, sl])
    ya_b = yt_ref[0].T.astype(BF16)
    _tail(x_ref[...], ya_b, d, sg_ref, p_ref, *w_refs, o_ref, yb_ref, alpha=alpha, ff_chunk=ff_chunk)


def _tail_sample_kernel(x_ref, ya_ref, u_ref, st_ref, sg_ref, p_ref, *rest, cnts, alpha, ff_chunk):
    *w_refs, o_ref, yb_ref = rest
    u = u_ref[...]
    d_pool = u.shape[1]
    pool_ch = d_pool // len(POOL_WINDOWS)
    d = []
    for g, w in enumerate(POOL_WINDOWS):
        pooled = u[:, g * pool_ch:(g + 1) * pool_ch]
        for s in range(1, w):
            pooled = pooled + st_ref[POOL_BUF - s][:, g * pool_ch:(g + 1) * pool_ch]
        d.append(pooled / cnts[g] - u[:, g * pool_ch:(g + 1) * pool_ch])
    _tail(x_ref[...], ya_ref[...].astype(BF16), d, sg_ref, p_ref, *w_refs, o_ref, yb_ref,
          alpha=alpha, ff_chunk=ff_chunk)


def _weight_specs(weights, index_map_factory):
    return [_resident(w.shape, index_map_factory(w.ndim)) for w in weights]


def _tail_prompt(x2d, yt, u, sg, p2d, weights, seq, alpha, tm=256, ff_chunk=1024):
    t, d_model = x2d.shape
    d_pool = u.shape[1]
    batch = t // seq
    nt = seq // tm
    hb = tm // POOL_HALO
    yt3 = yt.reshape(batch, yt.shape[1] * yt.shape[2], seq)
    in_specs = [
        pl.BlockSpec((tm, d_model), lambda i: (i, 0)),
        pl.BlockSpec((1, yt3.shape[1], tm), lambda i: (i // nt, 0, i % nt)),
        pl.BlockSpec((tm, d_pool), lambda i: (i, 0)),
        pl.BlockSpec((POOL_HALO, d_pool), lambda i: (jnp.maximum(i * hb - 1, 0), 0)),
        pl.BlockSpec((tm, sg.shape[1]), lambda i: (i, 0)),
        pl.BlockSpec((tm, p2d.shape[1]), lambda i: (i, 0)),
    ] + _weight_specs(weights, lambda nd: (lambda i: (0,) * nd))
    return pl.pallas_call(
        functools.partial(_tail_prompt_kernel, tiles_per_seq=nt, alpha=alpha, ff_chunk=ff_chunk),
        out_shape=jax.ShapeDtypeStruct((t, d_model), F32),
        grid=(t // tm,),
        in_specs=in_specs,
        out_specs=pl.BlockSpec((tm, d_model), lambda i: (i, 0)),
        scratch_shapes=[pltpu.VMEM((tm, d_pool), BF16), pltpu.VMEM((POOL_HALO + tm, d_pool), F32)],
        compiler_params=_params(("arbitrary",)),
        name="tail_prompt",
    )(x2d, yt3, u, u, sg, p2d, *weights)


def _tail_sample(x2d, ya, u, state_t, sg, p2d, weights, past_len, alpha, ff_chunk=1024):
    n, d_model = x2d.shape
    d_pool = u.shape[1]
    cnts = tuple(float(min(w, past_len + 1)) for w in POOL_WINDOWS)
    operands = (x2d, ya, u, state_t, sg, p2d)
    whole = lambda nd: (lambda i: (0,) * nd)
    in_specs = [pl.BlockSpec(a.shape, whole(a.ndim)) for a in operands] + _weight_specs(weights, whole)
    return pl.pallas_call(
        functools.partial(_tail_sample_kernel, cnts=cnts, alpha=alpha, ff_chunk=ff_chunk),
        out_shape=jax.ShapeDtypeStruct((n, d_model), F32),
        grid=(1,),
        in_specs=in_specs,
        out_specs=pl.BlockSpec((n, d_model), lambda i: (0, 0)),
        scratch_shapes=[pltpu.VMEM((n, d_pool), BF16)],
        compiler_params=_params(("arbitrary",)),
        name="tail_sample",
    )(*operands, *weights)


def _proj_sample_kernel(x_ref, w_ref, q_ref, k_ref, v_ref, u_ref, sg_ref, *, d_attn, d_pool):
    xb = x_ref[...].astype(BF16)
    z = _dot(xb, w_ref[...])
    q_ref[...] = z[:, 0:d_attn] * (HEAD_DIM ** -0.5)
    k_ref[...] = z[:, d_attn:2 * d_attn]
    v_ref[...] = z[:, 2 * d_attn:3 * d_attn]
    u_ref[...] = z[:, 3 * d_attn:3 * d_attn + d_pool]
    sg_ref[...] = _sigmoid(z[:, 3 * d_attn + d_pool:]).astype(BF16)


def _proj_sample(x2d, w_in):
    n, d_model = x2d.shape
    d_attn = d_model // 2
    d_pool = d_model // 2
    n_gate = w_in.shape[1] - 3 * d_attn - d_pool
    shapes = [(n, d_attn)] * 3 + [(n, d_pool)]
    return pl.pallas_call(
        functools.partial(_proj_sample_kernel, d_attn=d_attn, d_pool=d_pool),
        out_shape=tuple(jax.ShapeDtypeStruct(s, F32) for s in shapes)
        + (jax.ShapeDtypeStruct((n, n_gate), BF16),),
        grid=(1,),
        in_specs=[pl.BlockSpec(x2d.shape, lambda i: (0, 0)), pl.BlockSpec(w_in.shape, lambda i: (0, 0))],
        out_specs=tuple(pl.BlockSpec(s, lambda i: (0, 0)) for s in shapes)
        + (pl.BlockSpec((n, n_gate), lambda i: (0, 0)),),
        compiler_params=_params(("arbitrary",)),
        name="proj_sample",
    )(x2d, w_in)


SAMPLE_BLOCKS_PER_STEP = 8


def _select_sample_kernel(pt_ref, q_ref, *rest, pages_per_block, nblk):
    n_pages = SAMPLE_BLOCKS_PER_STEP * pages_per_block
    page_refs, (idx_ref, g_ref) = rest[:n_pages], rest[n_pages:]
    c = pl.program_id(1)
    n_heads, head_dim, page_rows = page_refs[0].shape[1:]
    d_attn = n_heads * head_dim
    lane = lax.broadcasted_iota(jnp.int32, (n_heads, LANES), 1)

    @pl.when(c == 0)
    def _():
        g_ref[...] = jnp.full((n_heads, LANES), -jnp.inf, F32)

    hrow = lax.broadcasted_iota(jnp.int32, (n_heads, d_attn), 0)
    hcol = lax.broadcasted_iota(jnp.int32, (n_heads, d_attn), 1) // head_dim
    qrows = jnp.where(hrow == hcol, q_ref[0], 0.0).astype(BF16)
    g = g_ref[...]
    for r in range(SAMPLE_BLOCKS_PER_STEP):
        tot = jnp.zeros((n_heads, page_rows), F32)
        for s in range(pages_per_block):
            page = page_refs[r * pages_per_block + s][0].reshape(d_attn, page_rows)
            tot = tot + _dot(qrows, page.astype(BF16))
        gate = jnp.sum(tot, axis=1, keepdims=True) * (1.0 / (page_rows * pages_per_block))
        g = jnp.where(lane == c * SAMPLE_BLOCKS_PER_STEP + r, gate, g)
    g_ref[...] = g

    @pl.when(c == pl.num_programs(1) - 1)
    def _():
        col = lane.astype(F32)
        gg = g
        out = jnp.zeros((n_heads, LANES), F32)
        for r in range(MOBA_TOPK):
            mx = jnp.max(gg, axis=1, keepdims=True)
            idx = jnp.min(jnp.where(gg == mx, col, float(LANES)), axis=1, keepdims=True)
            out = jnp.where(lane == r, idx, out)
            gg = jnp.where(col == idx, -jnp.inf, gg)
        idx_ref[0] = jnp.minimum(out.astype(jnp.int32), nblk - 1)


def _select_sample(page_table, q3, cache_kt):
    n, n_pages = page_table.shape
    n_heads, head_dim, page_rows = cache_kt.shape[1:]
    ppb = MOBA_BLOCK // page_rows
    nblk = n_pages // ppb
    assert nblk % SAMPLE_BLOCKS_PER_STEP == 0 and MOBA_TOPK <= nblk <= LANES
    pages_per_step = SAMPLE_BLOCKS_PER_STEP * ppb
    page_specs = [
        pl.BlockSpec((1, n_heads, head_dim, page_rows),
                     lambda b, c, pt, r=r: (pt[b, c * pages_per_step + r], 0, 0, 0))
        for r in range(pages_per_step)
    ]
    grid_spec = pltpu.PrefetchScalarGridSpec(
        num_scalar_prefetch=1,
        grid=(n, nblk // SAMPLE_BLOCKS_PER_STEP),
        in_specs=[pl.BlockSpec((1, 1, n_heads * head_dim), lambda b, c, pt: (b, 0, 0))] + page_specs,
        out_specs=pl.BlockSpec((1, n_heads, LANES), lambda b, c, pt: (b, 0, 0)),
        scratch_shapes=[pltpu.VMEM((n_heads, LANES), F32)],
    )
    return pl.pallas_call(
        functools.partial(_select_sample_kernel, pages_per_block=ppb, nblk=nblk),
        out_shape=jax.ShapeDtypeStruct((n, n_heads, LANES), jnp.int32),
        grid_spec=grid_spec,
        compiler_params=_params(("arbitrary", "arbitrary")),
        name="select_sample",
    )(page_table, q3, *([cache_kt] * pages_per_step))


def _attn_sample_kernel(pt_ref, sel_ref, slope_ref, q_ref, kn_ref, vn_ref, *rest, n_heads, pages_per_block,
                        past_len):
    n_sel = MOBA_TOPK * pages_per_block
    k_refs, v_refs, (o_ref,) = rest[:n_sel], rest[n_sel:2 * n_sel], rest[2 * n_sel:]
    b = pl.program_id(0)
    h = pl.program_id(1)
    head_dim, page_rows = k_refs[0].shape[2:]
    q = q_ref[0, 0].astype(BF16)
    q8 = jnp.broadcast_to(q, (8, head_dim))
    slope = slope_ref[0][:, 0:page_rows]
    kpos = lax.broadcasted_iota(jnp.int32, (1, page_rows), 1)

    s_self = jnp.sum(q.astype(F32) * kn_ref[0, 0].astype(BF16).astype(F32), axis=1, keepdims=True)
    scores = []
    m = s_self
    for r in range(MOBA_TOPK):
        blk = sel_ref[(b * n_heads + h) * MOBA_TOPK + r]
        for s in range(pages_per_block):
            kt = k_refs[r * pages_per_block + s][0, 0].astype(BF16)
            dist = past_len - (blk * MOBA_BLOCK + s * page_rows + kpos)
            sc = _dot(q8, kt)[0:1, :] - slope * dist.astype(F32)
            scores.append(sc)
            m = jnp.maximum(m, jnp.max(sc, axis=1, keepdims=True))
    p_self = jnp.exp(s_self - m)
    l = p_self
    acc = p_self * vn_ref[0, 0]
    for i, sc in enumerate(scores):
        p = jnp.exp(sc - m)
        l = l + jnp.sum(p, axis=1, keepdims=True)
        p8 = jnp.broadcast_to(p.astype(BF16), (8, page_rows))
        acc = acc + _dot_nt(p8, v_refs[i][0, 0].astype(BF16))[0:1, :]
    o_ref[0, 0] = acc / l


def _attn_sample(page_table, sel, slopes, q4, k4, v4, cache_kt, cache_vt, past_len):
    n, n_heads, _, head_dim = q4.shape
    page_rows = cache_kt.shape[3]
    ppb = MOBA_BLOCK // page_rows

    def page_spec(r, s):
        def imap(b, h, pt, sl):
            blk = sl[(b * n_heads + h) * MOBA_TOPK + r]
            return (pt[b, blk * ppb + s], h, 0, 0)
        return pl.BlockSpec((1, 1, head_dim, page_rows), imap)

    page_specs = [page_spec(r, s) for r in range(MOBA_TOPK) for s in range(ppb)]
    tok = pl.BlockSpec((1, 1, 1, head_dim), lambda b, h, pt, sl: (b, h, 0, 0))
    grid_spec = pltpu.PrefetchScalarGridSpec(
        num_scalar_prefetch=2,
        grid=(n, n_heads),
        in_specs=[pl.BlockSpec((1, 1, MOBA_BLOCK), lambda b, h, pt, sl: (h, 0, 0)), tok, tok, tok]
        + page_specs + page_specs,
        out_specs=tok,
    )
    n_sel = len(page_specs)
    return pl.pallas_call(
        functools.partial(_attn_sample_kernel, n_heads=n_heads, pages_per_block=ppb, past_len=past_len),
        out_shape=jax.ShapeDtypeStruct(q4.shape, F32),
        grid_spec=grid_spec,
        compiler_params=_params(("arbitrary", "arbitrary")),
        name="attn_sample",
    )(page_table, sel, slopes, q4, k4, v4, *([cache_kt] * n_sel), *([cache_vt] * n_sel))


def kernel(x_prompt, x_sample, cache_k, cache_v, state_pool, page_table, p_prompt, p_sample, w_in, w_branch,
           w_out, ln1_g, ln1_b, w_up, w_down, ln2_g, ln2_b, w_pe, w_pg, w_pool, pool_scale):
    depth = w_in.shape[0]
    batch, seq, d_model = x_prompt.shape
    n_dec, dec_seq, _ = x_sample.shape
    assert dec_seq == 1, "the sample path handles one new token per sequence"
    d_attn = d_model // 2
    d_pool = d_model // 2
    n_heads = d_attn // HEAD_DIM
    n_phys, page_rows = cache_k.shape[1:3]
    past_len = page_table.shape[1] * page_rows
    alpha = (2.0 * depth) ** 0.25
    nblk = seq // MOBA_BLOCK
    nrow = -(-nblk // 16) * 16
    assert past_len // MOBA_BLOCK >= MOBA_TOPK and page_rows <= LANES
    slopes = jnp.asarray([[[2.0 ** (-8.0 * (h + 1) / n_heads)] * MOBA_BLOCK] for h in range(n_heads)], F32)

    y_p = x_prompt.reshape(batch * seq, d_model)
    y_s = x_sample.reshape(n_dec, d_model)
    outs = [[] for _ in range(6)]
    for i in range(depth):
        w_in_b = w_in[i].astype(BF16)
        row = lambda a: a[i].reshape(1, -1)
        weights = (w_pool[i].astype(BF16), row(pool_scale), w_branch[i].astype(BF16), w_out[i].astype(BF16),
                   row(ln1_g), row(ln1_b), w_up[i].astype(BF16), w_down[i].astype(BF16),
                   row(ln2_g), row(ln2_b), w_pe[i].astype(BF16), w_pg[i].astype(BF16))

        kt, vtf, u, sg, qt, ka, vt, km = _proj_prompt(y_p, w_in_b, batch, seq)
        km = km.reshape(batch, nblk, n_heads, HEAD_DIM).transpose(0, 2, 1, 3)
        km = jnp.pad(km, ((0, 0), (0, 0), (0, nrow - nblk), (0, HEAD_DIM))).astype(BF16)
        yt = _attn_prompt(slopes, qt, ka, vt, km)
        y_p = _tail_prompt(y_p, yt, u, sg, p_prompt[i].reshape(batch * seq, -1), weights, seq, alpha)
        outs[0].append(kt.transpose(0, 3, 1, 2))
        outs[1].append(vtf.transpose(0, 3, 1, 2))
        outs[2].append(u.reshape(batch, seq, d_pool)[:, seq - POOL_BUF:])

        qs, ks, vs, us, sgs = _proj_sample(y_s, w_in_b)
        ckt = cache_k[i].transpose(0, 2, 3, 1)
        cvt = cache_v[i].transpose(0, 2, 3, 1)
        q4, k4, v4 = (a.reshape(n_dec, n_heads, 1, HEAD_DIM) for a in (qs, ks, vs))
        sel = _select_sample(page_table, qs.reshape(n_dec, 1, d_attn), ckt)[:, :, :MOBA_TOPK].reshape(-1)
        ya_s = _attn_sample(page_table, sel, slopes, q4, k4, v4, ckt, cvt, past_len).reshape(n_dec, d_attn)
        state = state_pool[i]
        y_s = _tail_sample(y_s, ya_s, us, state.transpose(1, 0, 2), sgs,
                           p_sample[i].reshape(n_dec, -1), weights, past_len, alpha)
        outs[3].append(ks.reshape(n_dec, 1, n_heads, HEAD_DIM))
        outs[4].append(vs.reshape(n_dec, 1, n_heads, HEAD_DIM))
        outs[5].append(jnp.concatenate([state[:, 1:], us[:, None, :]], axis=1))

    k_p, v_p, pool_p, k_s, v_s, pool_s = (jnp.stack(o) for o in outs)
    return (y_p.reshape(batch, seq, d_model), y_s.reshape(n_dec, 1, d_model), k_p, v_p, pool_p, k_s, v_s, pool_s)
```

```python
import functools

import jax
import jax.numpy as jnp
import numpy as np
from jax import lax
from jax.experimental import pallas as pl
from jax.experimental.pallas import tpu as pltpu

HEAD_DIM = 64
MOBA_BLOCK = 256
MOBA_TOPK = 3
POOL_WINDOWS = (2, 4, 8, 16)
POOL_BUF = max(POOL_WINDOWS) - 1
POOL_HALO = 16
LN_EPS = 1e-5
NEG_INF = -1e30
LANES = 128
VMEM_LIMIT_BYTES = 56 * 1024 * 1024

F32 = jnp.float32
BF16 = jnp.bfloat16


def _dot(a, b):
    return jnp.dot(a, b, preferred_element_type=F32)


def _dot_nt(a, b):
    return lax.dot_general(a, b, (((1,), (1,)), ((), ())), preferred_element_type=F32)


def _resident(block_shape, index_map):
    return pl.BlockSpec(block_shape, index_map, pipeline_mode=pl.Buffered(1))


def _params(semantics):
    return pltpu.CompilerParams(dimension_semantics=semantics, vmem_limit_bytes=VMEM_LIMIT_BYTES)


def _sigmoid(x):
    return 1.0 / (1.0 + jnp.exp(-x))


def _layer_norm(x, g, b):
    mu = jnp.mean(x, axis=-1, keepdims=True)
    xc = x - mu
    var = jnp.mean(xc * xc, axis=-1, keepdims=True)
    return xc * lax.rsqrt(var + LN_EPS) * g + b


LOG2E = 1.4426950408889634


def _alibi_slope(h, n_heads):
    return 2.0 ** (-8.0 * (h + 1) / n_heads)


def _split_bf16(c):
    hi = float(np.asarray(c, dtype=BF16))
    lo = float(np.asarray(c - hi, dtype=BF16))
    return hi, lo


def _proj_prompt_kernel(x_ref, w_ref, k_ref, v_ref, u_ref, sg_ref, qt_ref, ka_ref, vt_ref, km_ref,
                        *, d_attn, d_pool, n_heads):
    tm = x_ref.shape[0]
    xb = x_ref[...].astype(BF16)
    o_k, o_v, o_u, o_g = d_attn, 2 * d_attn, 3 * d_attn, 3 * d_attn + d_pool

    q = _dot(xb, w_ref[:, 0:o_k]) * (LOG2E * HEAD_DIM ** -0.5)
    qt = q.T
    row = lax.broadcasted_iota(jnp.int32, (HEAD_DIM, tm), 0)
    for h in range(n_heads):
        c_hi, c_lo = _split_bf16(LOG2E * _alibi_slope(h, n_heads))
        qt_ref[0, h, 0:HEAD_DIM, :] = qt[h * HEAD_DIM:(h + 1) * HEAD_DIM, :].astype(BF16)
        extra = jnp.where(row == 0, 16.0 * c_hi, jnp.where(row == 1, c_hi,
                          jnp.where(row == 2, 16.0 * c_lo, jnp.where(row == 3, c_lo, 0.0))))
        qt_ref[0, h, HEAD_DIM:2 * HEAD_DIM, :] = extra.astype(BF16)

    k = _dot(xb, w_ref[:, o_k:o_v])
    kt = k.T
    for h in range(n_heads):
        k_ref[0, h] = kt[h * HEAD_DIM:(h + 1) * HEAD_DIM, :]
    for c in range(tm // MOBA_BLOCK):
        blk = k[c * MOBA_BLOCK:(c + 1) * MOBA_BLOCK, :]
        km_ref[c] = jnp.sum(blk, axis=0, keepdims=True) * (1.0 / MOBA_BLOCK)
    prow = lax.broadcasted_iota(jnp.int32, (tm, LANES), 0) & (MOBA_BLOCK - 1)
    lane = lax.broadcasted_iota(jnp.int32, (tm, LANES), 1)
    sixteens = jnp.logical_or(lane == HEAD_DIM, lane == HEAD_DIM + 2)
    units = jnp.logical_or(lane == HEAD_DIM + 1, lane == HEAD_DIM + 3)
    posf = jnp.where(sixteens, (prow >> 4).astype(F32), jnp.where(units, (prow & 15).astype(F32), 0.0))
    low = lane < HEAD_DIM
    for m in range(n_heads // 2):
        col = k[:, m * LANES:(m + 1) * LANES]
        ka_ref[0, 2 * m] = jnp.where(low, col, posf).astype(BF16)
        ka_ref[0, 2 * m + 1] = jnp.where(low, pltpu.roll(col, HEAD_DIM, 1), posf).astype(BF16)

    vt = _dot(xb, w_ref[:, o_v:o_u]).T
    for h in range(n_heads):
        v_ref[0, h] = vt[h * HEAD_DIM:(h + 1) * HEAD_DIM, :]
        for c in range(tm // MOBA_BLOCK):
            vt_ref[0, h, c] = vt[h * HEAD_DIM:(h + 1) * HEAD_DIM,
                                 c * MOBA_BLOCK:(c + 1) * MOBA_BLOCK].astype(BF16)

    u_ref[...] = _dot(xb, w_ref[:, o_u:o_g])
    sg_ref[...] = _sigmoid(_dot(xb, w_ref[:, o_g:])).astype(BF16)


def _proj_prompt(x2d, w_in, batch, seq, tm=512):
    t, d_model = x2d.shape
    d_attn = d_model // 2
    d_pool = d_model // 2
    n_heads = d_attn // HEAD_DIM
    d_in = w_in.shape[1]
    n_gate = d_in - 3 * d_attn - d_pool
    nt = seq // tm
    nblk = seq // MOBA_BLOCK
    cb = tm // MOBA_BLOCK
    out_shape = (
        jax.ShapeDtypeStruct((batch, n_heads, HEAD_DIM, seq), F32),
        jax.ShapeDtypeStruct((batch, n_heads, HEAD_DIM, seq), F32),
        jax.ShapeDtypeStruct((t, d_pool), F32),
        jax.ShapeDtypeStruct((t, n_gate), BF16),
        jax.ShapeDtypeStruct((batch, n_heads, 2 * HEAD_DIM, seq), BF16),
        jax.ShapeDtypeStruct((batch, n_heads, seq, 2 * HEAD_DIM), BF16),
        jax.ShapeDtypeStruct((batch, n_heads, nblk, HEAD_DIM, MOBA_BLOCK), BF16),
        jax.ShapeDtypeStruct((t // MOBA_BLOCK, 1, d_attn), F32),
    )
    out_specs = (
        pl.BlockSpec((1, n_heads, HEAD_DIM, tm), lambda i: (i // nt, 0, 0, i % nt)),
        pl.BlockSpec((1, n_heads, HEAD_DIM, tm), lambda i: (i // nt, 0, 0, i % nt)),
        pl.BlockSpec((tm, d_pool), lambda i: (i, 0)),
        pl.BlockSpec((tm, n_gate), lambda i: (i, 0)),
        pl.BlockSpec((1, n_heads, 2 * HEAD_DIM, tm), lambda i: (i // nt, 0, 0, i % nt)),
        pl.BlockSpec((1, n_heads, tm, 2 * HEAD_DIM), lambda i: (i // nt, 0, i % nt, 0)),
        pl.BlockSpec((1, n_heads, cb, HEAD_DIM, MOBA_BLOCK), lambda i: (i // nt, 0, i % nt, 0, 0)),
        pl.BlockSpec((cb, 1, d_attn), lambda i: (i, 0, 0)),
    )
    return pl.pallas_call(
        functools.partial(_proj_prompt_kernel, d_attn=d_attn, d_pool=d_pool, n_heads=n_heads),
        out_shape=out_shape,
        grid=(t // tm,),
        in_specs=[pl.BlockSpec((tm, d_model), lambda i: (i, 0)),
                  _resident((d_model, d_in), lambda i: (0, 0))],
        out_specs=out_specs,
        compiler_params=_params(("arbitrary",)),
        name="proj_prompt",
    )(x2d, w_in)


def _attn_prompt_kernel(slope_ref, qt_ref, ka_ref, vt_ref, km_ref, o_ref,
                        mask_ref, m_ref, l_ref, acc_ref, s_ref, p_ref, a_ref):
    qi = pl.program_id(1)
    n_heads, nrow, tq = mask_ref.shape
    own = 2
    row = lax.broadcasted_iota(jnp.int32, (nrow, tq), 0).astype(F32)
    qif = jnp.full((1, tq), qi, jnp.int32).astype(F32)
    ik = lax.broadcasted_iota(jnp.int32, (MOBA_BLOCK, tq), 0)
    iq = lax.broadcasted_iota(jnp.int32, (MOBA_BLOCK, tq), 1)

    def scores(j, slot):
        off = pl.multiple_of(j * MOBA_BLOCK, MOBA_BLOCK)
        for h in range(n_heads):
            s_ref[slot, h] = _dot(ka_ref[0, h, pl.ds(off, MOBA_BLOCK), :], qt_ref[0, h])

    def softmax(j, slot):
        for h in range(n_heads):
            for c in range(tq // LANES):
                cols = slice(c * LANES, (c + 1) * LANES)
                s = s_ref[slot, h, :, cols]
                bias = mask_ref[h, pl.ds(j, 1), :][:, cols]
                m = m_ref[h, :, cols]
                m_new = jnp.maximum(m, jnp.max(s, axis=0, keepdims=True) + bias)
                alpha = jnp.exp2(m - m_new)
                p = jnp.exp2(s - (m_new - bias))
                m_ref[h, :, cols] = m_new
                l_ref[h, :, cols] = alpha * l_ref[h, :, cols] + jnp.sum(p, axis=0, keepdims=True)
                a_ref[slot, h, :, cols] = alpha
                p_ref[slot, h, :, cols] = p.astype(BF16)

    def accumulate(j, slot):
        for h in range(n_heads):
            acc_ref[h] = a_ref[slot, h] * acc_ref[h] + _dot(vt_ref[0, h, j], p_ref[slot, h])

    scores(0, 0)
    scores(1, 1)
    scores(qi, own)

    for h in range(n_heads):
        g = jnp.where(row < qif, _dot(km_ref[0, h], qt_ref[0, h]), NEG_INF)
        sel = jnp.zeros((nrow, tq), jnp.bool_)
        for r in range(MOBA_TOPK):
            mx = jnp.max(g, axis=0, keepdims=True)
            idx = jnp.min(jnp.where(g == mx, row, float(nrow)), axis=0, keepdims=True)
            hit = row == idx
            sel = jnp.logical_or(sel, jnp.logical_and(hit, qif > float(r)))
            g = jnp.where(hit, -jnp.inf, g)
        mask_ref[h] = jnp.where(sel, slope_ref[h] * float(MOBA_BLOCK) * (row - qif), NEG_INF)

    for h in range(n_heads):
        s = jnp.where(ik <= iq, s_ref[own, h], NEG_INF)
        m0 = jnp.max(s, axis=0, keepdims=True)
        p = jnp.exp2(s - m0)
        m_ref[h] = m0
        l_ref[h] = jnp.sum(p, axis=0, keepdims=True)
        p_ref[own, h] = p.astype(BF16)
    for h in range(n_heads):
        acc_ref[h] = _dot(vt_ref[0, h, qi], p_ref[own, h])

    softmax(0, 0)

    def body(jj, carry):
        j = 2 * jj
        scores(j, 0)
        softmax(j - 1, 1)
        accumulate(j - 2, 0)
        scores(j + 1, 1)
        softmax(j, 0)
        accumulate(j - 1, 1)
        return carry

    pairs = jnp.maximum((qi + 1) >> 1, 1)
    lax.fori_loop(1, pairs, body, 0)
    last = 2 * pairs - 1
    softmax(last, 1)
    accumulate(last - 1, 0)
    accumulate(last, 1)

    for h in range(n_heads):
        o_ref[0, h] = acc_ref[h] * (1.0 / l_ref[h])


def _attn_prompt(slopes, qt, ka, vt, km):
    batch, n_heads, _, seq = qt.shape
    nblk = seq // MOBA_BLOCK
    nrow = km.shape[2]
    per_batch = lambda nd: (lambda b, i: (b,) + (0,) * (nd - 1))
    return pl.pallas_call(
        _attn_prompt_kernel,
        out_shape=jax.ShapeDtypeStruct((batch, n_heads, HEAD_DIM, seq), F32),
        grid=(batch, nblk),
        in_specs=[
            pl.BlockSpec(slopes.shape, lambda b, i: (0, 0, 0)),
            pl.BlockSpec((1, n_heads, 2 * HEAD_DIM, MOBA_BLOCK), lambda b, i: (b, 0, 0, i)),
            _resident((1,) + ka.shape[1:], per_batch(4)),
            _resident((1,) + vt.shape[1:], per_batch(5)),
            pl.BlockSpec((1,) + km.shape[1:], per_batch(4)),
        ],
        out_specs=pl.BlockSpec((1, n_heads, HEAD_DIM, MOBA_BLOCK), lambda b, i: (b, 0, 0, i)),
        scratch_shapes=[pltpu.VMEM((n_heads, nrow, MOBA_BLOCK), F32),
                        pltpu.VMEM((n_heads, 1, MOBA_BLOCK), F32),
                        pltpu.VMEM((n_heads, 1, MOBA_BLOCK), F32),
                        pltpu.VMEM((n_heads, HEAD_DIM, MOBA_BLOCK), F32),
                        pltpu.VMEM((3, n_heads, MOBA_BLOCK, MOBA_BLOCK), F32),
                        pltpu.VMEM((3, n_heads, MOBA_BLOCK, MOBA_BLOCK), BF16),
                        pltpu.VMEM((2, n_heads, 1, MOBA_BLOCK), F32)],
        compiler_params=_params(("arbitrary", "arbitrary")),
        name="attn_prompt",
    )(slopes, qt, ka, vt, km)


def _tail(x, ya_b, d, sg_ref, p_ref, wp_ref, ps_ref, wb_ref, wo_ref, g1_ref, b1_ref, wu_ref, wd_ref,
          g2_ref, b2_ref, wpe_ref, wpg_ref, o_ref, yb_ref, *, alpha, ff_chunk):
    d_model = x.shape[1]
    pool_ch = wp_ref.shape[1]
    for g in range(len(POOL_WINDOWS)):
        sl = slice(g * pool_ch, (g + 1) * pool_ch)
        yb = _dot(d[g].astype(BF16), wp_ref[g]) * ps_ref[:, sl]
        yb_ref[:, sl] = yb.astype(BF16)
    br0 = _dot(ya_b, wb_ref[0])
    br1 = _dot(yb_ref[...], wb_ref[1])
    merged = sg_ref[:, 0:d_model].astype(F32) * br0 + sg_ref[:, d_model:].astype(F32) * br1
    mix = _dot(merged.astype(BF16), wo_ref[...])
    h1 = _layer_norm(alpha * x + mix, g1_ref[...], b1_ref[...])
    h1b = h1.astype(BF16)
    ple = _sigmoid(_dot(h1b, wpg_ref[...])) * _dot(p_ref[...].astype(BF16), wpe_ref[...])
    acc = alpha * h1 + ple
    d_ff = wu_ref.shape[1]
    for c in range(d_ff // ff_chunk):
        a = jnp.maximum(_dot(h1b, wu_ref[:, c * ff_chunk:(c + 1) * ff_chunk]), 0.0)
        acc = acc + _dot((a * a).astype(BF16), wd_ref[c * ff_chunk:(c + 1) * ff_chunk, :])
    o_ref[...] = _layer_norm(acc, g2_ref[...], b2_ref[...])


def _tail_prompt_kernel(x_ref, yt_ref, u_ref, halo_ref, sg_ref, p_ref, *rest, tiles_per_seq, alpha, ff_chunk):
    *w_refs, o_ref, yb_ref, ue_ref = rest
    tm = x_ref.shape[0]
    ti = pl.program_id(0) % tiles_per_seq
    u = u_ref[...]
    ue_ref[POOL_HALO:POOL_HALO + tm, :] = u

    @pl.when(ti == 0)
    def _():
        ue_ref[0:POOL_HALO, :] = jnp.zeros((POOL_HALO, u.shape[1]), F32)

    @pl.when(ti != 0)
    def _():
        ue_ref[0:POOL_HALO, :] = halo_ref[...]

    pool_ch = u.shape[1] // len(POOL_WINDOWS)
    pos = ti * tm + lax.broadcasted_iota(jnp.int32, (tm, pool_ch), 0)
    d = []
    for g, w in enumerate(POOL_WINDOWS):
        sl = slice(g * pool_ch, (g + 1) * pool_ch)
        pooled = u[:, sl]
        for s in range(1, w):
            pooled = pooled + ue_ref[POOL_HALO - s:POOL_HALO - s + tm, sl]
        cnt = jnp.minimum(w, pos + 1).astype(F32)
        d.append(pooled / cnt - u[:, sl])
    ya_b = yt_ref[0].T.astype(BF16)
    _tail(x_ref[...], ya_b, d, sg_ref, p_ref, *w_refs, o_ref, yb_ref, alpha=alpha, ff_chunk=ff_chunk)


def _tail_sample_kernel(x_ref, ya_ref, u_ref, st_ref, sg_ref, p_ref, *rest, cnts, alpha, ff_chunk):
    *w_refs, o_ref, yb_ref = rest
    u = u_ref[...]
    d_pool = u.shape[1]
    pool_ch = d_pool // len(POOL_WINDOWS)
    d = []
    for g, w in enumerate(POOL_WINDOWS):
        pooled = u[:, g * pool_ch:(g + 1) * pool_ch]
        for s in range(1, w):
            pooled = pooled + st_ref[POOL_BUF - s][:, g * pool_ch:(g + 1) * pool_ch]
        d.append(pooled / cnts[g] - u[:, g * pool_ch:(g + 1) * pool_ch])
    _tail(x_ref[...], ya_ref[...].astype(BF16), d, sg_ref, p_ref, *w_refs, o_ref, yb_ref,
          alpha=alpha, ff_chunk=ff_chunk)


def _weight_specs(weights, index_map_factory):
    return [_resident(w.shape, index_map_factory(w.ndim)) for w in weights]


def _tail_prompt(x2d, yt, u, sg, p2d, weights, seq, alpha, tm=256, ff_chunk=1024):
    t, d_model = x2d.shape
    d_pool = u.shape[1]
    batch = t // seq
    nt = seq // tm
    hb = tm // POOL_HALO
    yt3 = yt.reshape(batch, yt.shape[1] * yt.shape[2], seq)
    in_specs = [
        pl.BlockSpec((tm, d_model), lambda i: (i, 0)),
        pl.BlockSpec((1, yt3.shape[1], tm), lambda i: (i // nt, 0, i % nt)),
        pl.BlockSpec((tm, d_pool), lambda i: (i, 0)),
        pl.BlockSpec((POOL_HALO, d_pool), lambda i: (jnp.maximum(i * hb - 1, 0), 0)),
        pl.BlockSpec((tm, sg.shape[1]), lambda i: (i, 0)),
        pl.BlockSpec((tm, p2d.shape[1]), lambda i: (i, 0)),
    ] + _weight_specs(weights, lambda nd: (lambda i: (0,) * nd))
    return pl.pallas_call(
        functools.partial(_tail_prompt_kernel, tiles_per_seq=nt, alpha=alpha, ff_chunk=ff_chunk),
        out_shape=jax.ShapeDtypeStruct((t, d_model), F32),
        grid=(t // tm,),
        in_specs=in_specs,
        out_specs=pl.BlockSpec((tm, d_model), lambda i: (i, 0)),
        scratch_shapes=[pltpu.VMEM((tm, d_pool), BF16), pltpu.VMEM((POOL_HALO + tm, d_pool), F32)],
        compiler_params=_params(("arbitrary",)),
        name="tail_prompt",
    )(x2d, yt3, u, u, sg, p2d, *weights)


def _tail_sample(x2d, ya, u, state_t, sg, p2d, weights, past_len, alpha, ff_chunk=1024):
    n, d_model = x2d.shape
    d_pool = u.shape[1]
    cnts = tuple(float(min(w, past_len + 1)) for w in POOL_WINDOWS)
    operands = (x2d, ya, u, state_t, sg, p2d)
    whole = lambda nd: (lambda i: (0,) * nd)
    in_specs = [pl.BlockSpec(a.shape, whole(a.ndim)) for a in operands] + _weight_specs(weights, whole)
    return pl.pallas_call(
        functools.partial(_tail_sample_kernel, cnts=cnts, alpha=alpha, ff_chunk=ff_chunk),
        out_shape=jax.ShapeDtypeStruct((n, d_model), F32),
        grid=(1,),
        in_specs=in_specs,
        out_specs=pl.BlockSpec((n, d_model), lambda i: (0, 0)),
        scratch_shapes=[pltpu.VMEM((n, d_pool), BF16)],
        compiler_params=_params(("arbitrary",)),
        name="tail_sample",
    )(*operands, *weights)


def _proj_sample_kernel(x_ref, w_ref, wq_ref, q_ref, k_ref, v_ref, u_ref, sg_ref, qg_ref, *, d_attn, d_pool):
    x = x_ref[...]
    z = _dot(x.astype(BF16), w_ref[...])
    q_ref[...] = z[:, 0:d_attn] * (HEAD_DIM ** -0.5)
    k_ref[...] = z[:, d_attn:2 * d_attn]
    v_ref[...] = z[:, 2 * d_attn:3 * d_attn]
    u_ref[...] = z[:, 3 * d_attn:3 * d_attn + d_pool]
    sg_ref[...] = _sigmoid(z[:, 3 * d_attn + d_pool:]).astype(BF16)
    qg_ref[...] = jnp.dot(x, wq_ref[...], precision=lax.Precision.HIGHEST, preferred_element_type=F32)


def _proj_sample(x2d, w_in, wq_f32):
    n, d_model = x2d.shape
    d_attn = d_model // 2
    d_pool = d_model // 2
    n_gate = w_in.shape[1] - 3 * d_attn - d_pool
    shapes = [(n, d_attn)] * 3 + [(n, d_pool)]
    whole = lambda s: pl.BlockSpec(s, lambda i: (0, 0))
    return pl.pallas_call(
        functools.partial(_proj_sample_kernel, d_attn=d_attn, d_pool=d_pool),
        out_shape=tuple(jax.ShapeDtypeStruct(s, F32) for s in shapes)
        + (jax.ShapeDtypeStruct((n, n_gate), BF16), jax.ShapeDtypeStruct((n, d_attn), F32)),
        grid=(1,),
        in_specs=[whole(x2d.shape), whole(w_in.shape), whole(wq_f32.shape)],
        out_specs=tuple(whole(s) for s in shapes) + (whole((n, n_gate)), whole((n, d_attn))),
        compiler_params=_params(("arbitrary",)),
        name="proj_sample",
    )(x2d, w_in, wq_f32)


SAMPLE_BLOCKS_PER_STEP = 8


def _select_sample_kernel(pt_ref, q_ref, *rest, pages_per_block, nblk):
    n_pages = SAMPLE_BLOCKS_PER_STEP * pages_per_block
    page_refs, (idx_ref, g_ref) = rest[:n_pages], rest[n_pages:]
    c = pl.program_id(1)
    n_heads, head_dim, page_rows = page_refs[0].shape[1:]
    lane = lax.broadcasted_iota(jnp.int32, (n_heads, LANES), 1)

    @pl.when(c == 0)
    def _():
        g_ref[...] = jnp.full((n_heads, LANES), -jnp.inf, F32)

    qcol = jnp.broadcast_to(q_ref[0], (n_heads * head_dim, page_rows)).reshape(n_heads, head_dim, page_rows)
    g = g_ref[...]
    for r in range(SAMPLE_BLOCKS_PER_STEP):
        ksum = page_refs[r * pages_per_block][0]
        for s in range(1, pages_per_block):
            ksum = ksum + page_refs[r * pages_per_block + s][0]
        per_key = jnp.sum(ksum * qcol, axis=1)
        gate = jnp.sum(per_key, axis=1, keepdims=True) * (1.0 / (page_rows * pages_per_block))
        g = jnp.where(lane == c * SAMPLE_BLOCKS_PER_STEP + r, gate, g)
    g_ref[...] = g

    @pl.when(c == pl.num_programs(1) - 1)
    def _():
        col = lane.astype(F32)
        gg = g
        out = jnp.zeros((n_heads, LANES), F32)
        for r in range(MOBA_TOPK):
            mx = jnp.max(gg, axis=1, keepdims=True)
            idx = jnp.min(jnp.where(gg == mx, col, float(LANES)), axis=1, keepdims=True)
            out = jnp.where(lane == r, idx, out)
            gg = jnp.where(col == idx, -jnp.inf, gg)
        idx_ref[0] = jnp.minimum(out.astype(jnp.int32), nblk - 1)


def _select_sample(page_table, qcol, cache_kt):
    n, n_pages = page_table.shape
    n_heads, head_dim, page_rows = cache_kt.shape[1:]
    ppb = MOBA_BLOCK // page_rows
    nblk = n_pages // ppb
    assert nblk % SAMPLE_BLOCKS_PER_STEP == 0 and MOBA_TOPK <= nblk <= LANES
    pages_per_step = SAMPLE_BLOCKS_PER_STEP * ppb
    page_specs = [
        pl.BlockSpec((1, n_heads, head_dim, page_rows),
                     lambda b, c, pt, r=r: (pt[b, c * pages_per_step + r], 0, 0, 0))
        for r in range(pages_per_step)
    ]
    grid_spec = pltpu.PrefetchScalarGridSpec(
        num_scalar_prefetch=1,
        grid=(n, nblk // SAMPLE_BLOCKS_PER_STEP),
        in_specs=[pl.BlockSpec((1, n_heads * head_dim, 1), lambda b, c, pt: (b, 0, 0))] + page_specs,
        out_specs=pl.BlockSpec((1, n_heads, LANES), lambda b, c, pt: (b, 0, 0)),
        scratch_shapes=[pltpu.VMEM((n_heads, LANES), F32)],
    )
    return pl.pallas_call(
        functools.partial(_select_sample_kernel, pages_per_block=ppb, nblk=nblk),
        out_shape=jax.ShapeDtypeStruct((n, n_heads, LANES), jnp.int32),
        grid_spec=grid_spec,
        compiler_params=_params(("arbitrary", "arbitrary")),
        name="select_sample",
    )(page_table, qcol, *([cache_kt] * pages_per_step))


def _attn_sample_kernel(pt_ref, sel_ref, slope_ref, q_ref, kn_ref, vn_ref, *rest, n_heads, pages_per_block,
                        past_len):
    n_sel = MOBA_TOPK * pages_per_block
    k_refs, v_refs, (o_ref,) = rest[:n_sel], rest[n_sel:2 * n_sel], rest[2 * n_sel:]
    b = pl.program_id(0)
    h = pl.program_id(1)
    head_dim, page_rows = k_refs[0].shape[2:]
    q = q_ref[0, 0].astype(BF16)
    q8 = jnp.broadcast_to(q, (8, head_dim))
    slope = slope_ref[0][:, 0:page_rows]
    kpos = lax.broadcasted_iota(jnp.int32, (1, page_rows), 1)

    s_self = jnp.sum(q.astype(F32) * kn_ref[0, 0].astype(BF16).astype(F32), axis=1, keepdims=True)
    scores = []
    m = s_self
    for r in range(MOBA_TOPK):
        blk = sel_ref[(b * n_heads + h) * MOBA_TOPK + r]
        for s in range(pages_per_block):
            kt = k_refs[r * pages_per_block + s][0, 0].astype(BF16)
            dist = past_len - (blk * MOBA_BLOCK + s * page_rows + kpos)
            sc = _dot(q8, kt)[0:1, :] - slope * dist.astype(F32)
            scores.append(sc)
            m = jnp.maximum(m, jnp.max(sc, axis=1, keepdims=True))
    p_self = jnp.exp(s_self - m)
    l = p_self
    acc = p_self * vn_ref[0, 0]
    for i, sc in enumerate(scores):
        p = jnp.exp(sc - m)
        l = l + jnp.sum(p, axis=1, keepdims=True)
        p8 = jnp.broadcast_to(p.astype(BF16), (8, page_rows))
        acc = acc + _dot_nt(p8, v_refs[i][0, 0].astype(BF16))[0:1, :]
    o_ref[0, 0] = acc / l


def _attn_sample(page_table, sel, slopes, q4, k4, v4, cache_kt, cache_vt, past_len):
    n, n_heads, _, head_dim = q4.shape
    page_rows = cache_kt.shape[3]
    ppb = MOBA_BLOCK // page_rows

    def page_spec(r, s):
        def imap(b, h, pt, sl):
            blk = sl[(b * n_heads + h) * MOBA_TOPK + r]
            return (pt[b, blk * ppb + s], h, 0, 0)
        return pl.BlockSpec((1, 1, head_dim, page_rows), imap)

    page_specs = [page_spec(r, s) for r in range(MOBA_TOPK) for s in range(ppb)]
    tok = pl.BlockSpec((1, 1, 1, head_dim), lambda b, h, pt, sl: (b, h, 0, 0))
    grid_spec = pltpu.PrefetchScalarGridSpec(
        num_scalar_prefetch=2,
        grid=(n, n_heads),
        in_specs=[pl.BlockSpec((1, 1, MOBA_BLOCK), lambda b, h, pt, sl: (h, 0, 0)), tok, tok, tok]
        + page_specs + page_specs,
        out_specs=tok,
    )
    n_sel = len(page_specs)
    return pl.pallas_call(
        functools.partial(_attn_sample_kernel, n_heads=n_heads, pages_per_block=ppb, past_len=past_len),
        out_shape=jax.ShapeDtypeStruct(q4.shape, F32),
        grid_spec=grid_spec,
        compiler_params=_params(("arbitrary", "arbitrary")),
        name="attn_sample",
    )(page_table, sel, slopes, q4, k4, v4, *([cache_kt] * n_sel), *([cache_vt] * n_sel))


def kernel(x_prompt, x_sample, cache_k, cache_v, state_pool, page_table, p_prompt, p_sample, w_in, w_branch,
           w_out, ln1_g, ln1_b, w_up, w_down, ln2_g, ln2_b, w_pe, w_pg, w_pool, pool_scale):
    depth = w_in.shape[0]
    batch, seq, d_model = x_prompt.shape
    n_dec, dec_seq, _ = x_sample.shape
    assert dec_seq == 1, "the sample path handles one new token per sequence"
    d_attn = d_model // 2
    d_pool = d_model // 2
    n_heads = d_attn // HEAD_DIM
    n_phys, page_rows = cache_k.shape[1:3]
    past_len = page_table.shape[1] * page_rows
    alpha = (2.0 * depth) ** 0.25
    nblk = seq // MOBA_BLOCK
    nrow = -(-nblk // 16) * 16
    assert past_len // MOBA_BLOCK >= MOBA_TOPK and page_rows <= LANES
    slopes = jnp.asarray([[[_alibi_slope(h, n_heads)] * MOBA_BLOCK] for h in range(n_heads)], F32)

    y_p = x_prompt.reshape(batch * seq, d_model)
    y_s = x_sample.reshape(n_dec, d_model)
    outs = [[] for _ in range(6)]
    for i in range(depth):
        w_in_b = w_in[i].astype(BF16)
        row = lambda a: a[i].reshape(1, -1)
        weights = (w_pool[i].astype(BF16), row(pool_scale), w_branch[i].astype(BF16), w_out[i].astype(BF16),
                   row(ln1_g), row(ln1_b), w_up[i].astype(BF16), w_down[i].astype(BF16),
                   row(ln2_g), row(ln2_b), w_pe[i].astype(BF16), w_pg[i].astype(BF16))

        kt, vtf, u, sg, qt, ka, vt, km = _proj_prompt(y_p, w_in_b, batch, seq)
        km = km.reshape(batch, nblk, n_heads, HEAD_DIM).transpose(0, 2, 1, 3)
        km = jnp.pad(km, ((0, 0), (0, 0), (0, nrow - nblk), (0, HEAD_DIM))).astype(BF16)
        yt = _attn_prompt(slopes * LOG2E, qt, ka, vt, km)
        y_p = _tail_prompt(y_p, yt, u, sg, p_prompt[i].reshape(batch * seq, -1), weights, seq, alpha)
        outs[0].append(kt.transpose(0, 3, 1, 2))
        outs[1].append(vtf.transpose(0, 3, 1, 2))
        outs[2].append(u.reshape(batch, seq, d_pool)[:, seq - POOL_BUF:])

        qs, ks, vs, us, sgs, qg = _proj_sample(y_s, w_in_b, w_in[i][:, :d_attn])
        ckt = cache_k[i].transpose(0, 2, 3, 1)
        cvt = cache_v[i].transpose(0, 2, 3, 1)
        q4, k4, v4 = (a.reshape(n_dec, n_heads, 1, HEAD_DIM) for a in (qs, ks, vs))
        sel = _select_sample(page_table, qg.reshape(n_dec, d_attn, 1), ckt)[:, :, :MOBA_TOPK].reshape(-1)
        ya_s = _attn_sample(page_table, sel, slopes, q4, k4, v4, ckt, cvt, past_len).reshape(n_dec, d_attn)
        state = state_pool[i]
        y_s = _tail_sample(y_s, ya_s, us, state.transpose(1, 0, 2), sgs,
                           p_sample[i].reshape(n_dec, -1), weights, past_len, alpha)
        outs[3].append(ks.reshape(n_dec, 1, n_heads, HEAD_DIM))
        outs[4].append(vs.reshape(n_dec, 1, n_heads, HEAD_DIM))
        outs[5].append(jnp.concatenate([state[:, 1:], us[:, None, :]], axis=1))

    k_p, v_p, pool_p, k_s, v_s, pool_s = (jnp.stack(o) for o in outs)
    return (y_p.reshape(batch, seq, d_model), y_s.reshape(n_dec, 1, d_model), k_p, v_p, pool_p, k_s, v_s, pool_s)
```

```python
import functools

import jax
import jax.numpy as jnp
import numpy as np
from jax import lax
from jax.experimental import pallas as pl
from jax.experimental.pallas import tpu as pltpu

HEAD_DIM = 64
MOBA_BLOCK = 256
MOBA_TOPK = 3
POOL_WINDOWS = (2, 4, 8, 16)
POOL_BUF = max(POOL_WINDOWS) - 1
POOL_HALO = 16
LN_EPS = 1e-5
NEG_INF = -1e30
LANES = 128
VMEM_LIMIT_BYTES = 56 * 1024 * 1024

F32 = jnp.float32
BF16 = jnp.bfloat16


def _dot(a, b):
    return jnp.dot(a, b, preferred_element_type=F32)


def _dot_nt(a, b):
    return lax.dot_general(a, b, (((1,), (1,)), ((), ())), preferred_element_type=F32)


def _resident(block_shape, index_map):
    return pl.BlockSpec(block_shape, index_map, pipeline_mode=pl.Buffered(1))


def _params(semantics, flags=None):
    return pltpu.CompilerParams(dimension_semantics=semantics, vmem_limit_bytes=VMEM_LIMIT_BYTES, flags=flags)


def _sigmoid(x):
    return 1.0 / (1.0 + jnp.exp(-x))


def _layer_norm(x, g, b):
    mu = jnp.mean(x, axis=-1, keepdims=True)
    xc = x - mu
    var = jnp.mean(xc * xc, axis=-1, keepdims=True)
    return xc * lax.rsqrt(var + LN_EPS) * g + b


LOG2E = 1.4426950408889634
V_PAD = 16


def _alibi_slope(h, n_heads):
    return 2.0 ** (-8.0 * (h + 1) / n_heads)


def _split_bf16(c):
    hi = float(np.asarray(c, dtype=BF16))
    lo = float(np.asarray(c - hi, dtype=BF16))
    return hi, lo


def _proj_prompt_kernel(x_ref, w_ref, k_ref, v_ref, u_ref, sg_ref, qt_ref, ka_ref, vt_ref, km_ref,
                        *, d_attn, d_pool, n_heads):
    tm = x_ref.shape[0]
    xb = x_ref[...].astype(BF16)
    o_k, o_v, o_u, o_g = d_attn, 2 * d_attn, 3 * d_attn, 3 * d_attn + d_pool

    q = _dot(xb, w_ref[:, 0:o_k]) * (LOG2E * HEAD_DIM ** -0.5)
    qt = q.T
    row = lax.broadcasted_iota(jnp.int32, (HEAD_DIM, tm), 0)
    for h in range(n_heads):
        c_hi, c_lo = _split_bf16(LOG2E * _alibi_slope(h, n_heads))
        qt_ref[0, h, 0:HEAD_DIM, :] = qt[h * HEAD_DIM:(h + 1) * HEAD_DIM, :].astype(BF16)
        extra = jnp.where(row == 0, 16.0 * c_hi, jnp.where(row == 1, c_hi,
                          jnp.where(row == 2, 16.0 * c_lo, jnp.where(row == 3, c_lo, 0.0))))
        qt_ref[0, h, HEAD_DIM:2 * HEAD_DIM, :] = extra.astype(BF16)

    k = _dot(xb, w_ref[:, o_k:o_v])
    kt = k.T
    for h in range(n_heads):
        k_ref[0, h] = kt[h * HEAD_DIM:(h + 1) * HEAD_DIM, :]
    for c in range(tm // MOBA_BLOCK):
        blk = k[c * MOBA_BLOCK:(c + 1) * MOBA_BLOCK, :]
        km_ref[c] = jnp.sum(blk, axis=0, keepdims=True) * (1.0 / MOBA_BLOCK)
    prow = lax.broadcasted_iota(jnp.int32, (tm, LANES), 0) & (MOBA_BLOCK - 1)
    lane = lax.broadcasted_iota(jnp.int32, (tm, LANES), 1)
    sixteens = jnp.logical_or(lane == HEAD_DIM, lane == HEAD_DIM + 2)
    units = jnp.logical_or(lane == HEAD_DIM + 1, lane == HEAD_DIM + 3)
    posf = jnp.where(sixteens, (prow >> 4).astype(F32), jnp.where(units, (prow & 15).astype(F32), 0.0))
    low = lane < HEAD_DIM
    for m in range(n_heads // 2):
        col = k[:, m * LANES:(m + 1) * LANES]
        ka_ref[0, 2 * m] = jnp.where(low, col, posf).astype(BF16)
        ka_ref[0, 2 * m + 1] = jnp.where(low, pltpu.roll(col, HEAD_DIM, 1), posf).astype(BF16)

    vt = _dot(xb, w_ref[:, o_v:o_u]).T
    ones_row = jnp.where(lax.broadcasted_iota(jnp.int32, (V_PAD, MOBA_BLOCK), 0) == 0, 1.0, 0.0).astype(BF16)
    for h in range(n_heads):
        v_ref[0, h] = vt[h * HEAD_DIM:(h + 1) * HEAD_DIM, :]
        for c in range(tm // MOBA_BLOCK):
            vt_ref[0, h, c, 0:HEAD_DIM, :] = vt[h * HEAD_DIM:(h + 1) * HEAD_DIM,
                                                c * MOBA_BLOCK:(c + 1) * MOBA_BLOCK].astype(BF16)
            vt_ref[0, h, c, HEAD_DIM:HEAD_DIM + V_PAD, :] = ones_row

    u_ref[...] = _dot(xb, w_ref[:, o_u:o_g])
    sg_ref[...] = _sigmoid(_dot(xb, w_ref[:, o_g:])).astype(BF16)


def _proj_prompt(x2d, w_in, batch, seq, tm=512):
    t, d_model = x2d.shape
    d_attn = d_model // 2
    d_pool = d_model // 2
    n_heads = d_attn // HEAD_DIM
    d_in = w_in.shape[1]
    n_gate = d_in - 3 * d_attn - d_pool
    nt = seq // tm
    nblk = seq // MOBA_BLOCK
    cb = tm // MOBA_BLOCK
    out_shape = (
        jax.ShapeDtypeStruct((batch, n_heads, HEAD_DIM, seq), F32),
        jax.ShapeDtypeStruct((batch, n_heads, HEAD_DIM, seq), F32),
        jax.ShapeDtypeStruct((t, d_pool), F32),
        jax.ShapeDtypeStruct((t, n_gate), BF16),
        jax.ShapeDtypeStruct((batch, n_heads, 2 * HEAD_DIM, seq), BF16),
        jax.ShapeDtypeStruct((batch, n_heads, seq, 2 * HEAD_DIM), BF16),
        jax.ShapeDtypeStruct((batch, n_heads, nblk, HEAD_DIM + V_PAD, MOBA_BLOCK), BF16),
        jax.ShapeDtypeStruct((t // MOBA_BLOCK, 1, d_attn), F32),
    )
    out_specs = (
        pl.BlockSpec((1, n_heads, HEAD_DIM, tm), lambda i: (i // nt, 0, 0, i % nt)),
        pl.BlockSpec((1, n_heads, HEAD_DIM, tm), lambda i: (i // nt, 0, 0, i % nt)),
        pl.BlockSpec((tm, d_pool), lambda i: (i, 0)),
        pl.BlockSpec((tm, n_gate), lambda i: (i, 0)),
        pl.BlockSpec((1, n_heads, 2 * HEAD_DIM, tm), lambda i: (i // nt, 0, 0, i % nt)),
        pl.BlockSpec((1, n_heads, tm, 2 * HEAD_DIM), lambda i: (i // nt, 0, i % nt, 0)),
        pl.BlockSpec((1, n_heads, cb, HEAD_DIM + V_PAD, MOBA_BLOCK), lambda i: (i // nt, 0, i % nt, 0, 0)),
        pl.BlockSpec((cb, 1, d_attn), lambda i: (i, 0, 0)),
    )
    return pl.pallas_call(
        functools.partial(_proj_prompt_kernel, d_attn=d_attn, d_pool=d_pool, n_heads=n_heads),
        out_shape=out_shape,
        grid=(t // tm,),
        in_specs=[pl.BlockSpec((tm, d_model), lambda i: (i, 0)),
                  _resident((d_model, d_in), lambda i: (0, 0))],
        out_specs=out_specs,
        compiler_params=_params(("arbitrary",)),
        name="proj_prompt",
    )(x2d, w_in)


def _attn_prompt_kernel(slope_ref, qt_ref, ka_ref, vt_ref, km_ref, o_ref, mask_ref, m_ref, acc_ref, s_ref, p_ref):
    qi = pl.program_id(1)
    n_heads, nrow, tq = mask_ref.shape
    own = 2
    row = lax.broadcasted_iota(jnp.int32, (nrow, tq), 0).astype(F32)
    qif = jnp.full((1, tq), qi, jnp.int32).astype(F32)
    ik = lax.broadcasted_iota(jnp.int32, (MOBA_BLOCK, tq), 0)
    iq = lax.broadcasted_iota(jnp.int32, (MOBA_BLOCK, tq), 1)

    def scores(j, slot):
        off = pl.multiple_of(j * MOBA_BLOCK, MOBA_BLOCK)
        for h in range(n_heads):
            s_ref[slot, h] = _dot(ka_ref[0, h, pl.ds(off, MOBA_BLOCK), :], qt_ref[0, h])

    def absorb(j, slot):
        for h in range(n_heads):
            ps, alphas = [], []
            for c in range(tq // LANES):
                cols = slice(c * LANES, (c + 1) * LANES)
                s = s_ref[slot, h, :, cols]
                bias = mask_ref[h, pl.ds(j, 1), :][:, cols]
                m = m_ref[h, :, cols]
                m_new = jnp.maximum(m, jnp.max(s, axis=0, keepdims=True) + bias)
                alphas.append(jnp.exp2(m - m_new))
                ps.append(jnp.exp2(s - (m_new - bias)).astype(BF16))
                m_ref[h, :, cols] = m_new
            p = jnp.concatenate(ps, axis=1)
            acc_ref[h] = jnp.concatenate(alphas, axis=1) * acc_ref[h] + _dot(vt_ref[0, h, j], p)

    scores(0, 0)
    scores(1, 1)
    scores(qi, own)

    for h in range(n_heads):
        g = jnp.where(row < qif, _dot(km_ref[0, h], qt_ref[0, h]), NEG_INF)
        sel = jnp.zeros((nrow, tq), jnp.bool_)
        for r in range(MOBA_TOPK):
            mx = jnp.max(g, axis=0, keepdims=True)
            idx = jnp.min(jnp.where(g == mx, row, float(nrow)), axis=0, keepdims=True)
            hit = row == idx
            sel = jnp.logical_or(sel, jnp.logical_and(hit, qif > float(r)))
            g = jnp.where(hit, -jnp.inf, g)
        mask_ref[h] = jnp.where(sel, slope_ref[h] * float(MOBA_BLOCK) * (row - qif), NEG_INF)

    for h in range(n_heads):
        s = jnp.where(ik <= iq, s_ref[own, h], NEG_INF)
        m0 = jnp.max(s, axis=0, keepdims=True)
        p = jnp.exp2(s - m0)
        m_ref[h] = m0
        p_ref[h] = p.astype(BF16)
    for h in range(n_heads):
        acc_ref[h] = _dot(vt_ref[0, h, qi], p_ref[h])

    absorb(0, 0)

    def body(jj, carry):
        j = 2 * jj
        scores(j, 0)
        absorb(j - 1, 1)
        scores(j + 1, 1)
        absorb(j, 0)
        return carry

    pairs = jnp.maximum((qi + 1) >> 1, 1)
    lax.fori_loop(1, pairs, body, 0)
    last = 2 * pairs - 1
    absorb(last, 1)

    for h in range(n_heads):
        o_ref[0, h] = acc_ref[h, 0:HEAD_DIM, :] * (1.0 / acc_ref[h, HEAD_DIM:HEAD_DIM + 1, :])


def _attn_prompt(slopes, qt, ka, vt, km):
    batch, n_heads, _, seq = qt.shape
    nblk = seq // MOBA_BLOCK
    nrow = km.shape[2]
    per_batch = lambda nd: (lambda b, i: (b,) + (0,) * (nd - 1))
    return pl.pallas_call(
        _attn_prompt_kernel,
        out_shape=jax.ShapeDtypeStruct((batch, n_heads, HEAD_DIM, seq), F32),
        grid=(batch, nblk),
        in_specs=[
            pl.BlockSpec(slopes.shape, lambda b, i: (0, 0, 0)),
            pl.BlockSpec((1, n_heads, 2 * HEAD_DIM, MOBA_BLOCK), lambda b, i: (b, 0, 0, i)),
            _resident((1,) + ka.shape[1:], per_batch(4)),
            _resident((1,) + vt.shape[1:], per_batch(5)),
            pl.BlockSpec((1,) + km.shape[1:], per_batch(4)),
        ],
        out_specs=pl.BlockSpec((1, n_heads, HEAD_DIM, MOBA_BLOCK), lambda b, i: (b, 0, 0, i)),
        scratch_shapes=[pltpu.VMEM((n_heads, nrow, MOBA_BLOCK), F32),
                        pltpu.VMEM((n_heads, 1, MOBA_BLOCK), F32),
                        pltpu.VMEM((n_heads, HEAD_DIM + V_PAD, MOBA_BLOCK), F32),
                        pltpu.VMEM((3, n_heads, MOBA_BLOCK, MOBA_BLOCK), F32),
                        pltpu.VMEM((n_heads, MOBA_BLOCK, MOBA_BLOCK), BF16)],
        compiler_params=_params(("arbitrary", "arbitrary")),
        name="attn_prompt",
    )(slopes, qt, ka, vt, km)


def _tail(x, ya_b, d, sg_ref, p_ref, wp_ref, ps_ref, wb_ref, wo_ref, g1_ref, b1_ref, wu_ref, wd_ref,
          g2_ref, b2_ref, wpe_ref, wpg_ref, o_ref, yb_ref, *, alpha, ff_chunk, side_work=()):
    d_model = x.shape[1]
    pool_ch = wp_ref.shape[1]
    for g in range(len(POOL_WINDOWS)):
        sl = slice(g * pool_ch, (g + 1) * pool_ch)
        yb = _dot(d[g].astype(BF16), wp_ref[g]) * ps_ref[:, sl]
        yb_ref[:, sl] = yb.astype(BF16)
    br0 = _dot(ya_b, wb_ref[0])
    br1 = _dot(yb_ref[...], wb_ref[1])
    merged = sg_ref[:, 0:d_model].astype(F32) * br0 + sg_ref[:, d_model:].astype(F32) * br1
    mix = _dot(merged.astype(BF16), wo_ref[...])
    h1 = _layer_norm(alpha * x + mix, g1_ref[...], b1_ref[...])
    h1b = h1.astype(BF16)
    ple = _sigmoid(_dot(h1b, wpg_ref[...])) * _dot(p_ref[...].astype(BF16), wpe_ref[...])
    acc = alpha * h1 + ple
    d_ff = wu_ref.shape[1]
    assert len(side_work) <= d_ff // ff_chunk
    for c in range(d_ff // ff_chunk):
        if c < len(side_work):
            side_work[c]()
        a = jnp.maximum(_dot(h1b, wu_ref[:, c * ff_chunk:(c + 1) * ff_chunk]), 0.0)
        acc = acc + _dot((a * a).astype(BF16), wd_ref[c * ff_chunk:(c + 1) * ff_chunk, :])
    o_ref[...] = _layer_norm(acc, g2_ref[...], b2_ref[...])


def _rank_sample_jobs(page_refs, q_ref, g_ref, idx_ref, chunk, pages_per_block, nblk, n_jobs):
    n_heads, head_dim, page_rows = page_refs[0].shape[1:]
    blocks = len(page_refs) // pages_per_block
    assert blocks % n_jobs == 0
    per_job = blocks // n_jobs

    def job(k):
        lane = lax.broadcasted_iota(jnp.int32, (n_heads, LANES), 1)
        qcol = jnp.broadcast_to(q_ref[0], (n_heads * head_dim, page_rows)).reshape(n_heads, head_dim, page_rows)
        g = g_ref[...]
        if k == 0:
            g = jnp.where(chunk == 0, -jnp.inf, g)
        for r in range(k * per_job, (k + 1) * per_job):
            ksum = page_refs[r * pages_per_block][0]
            for s in range(1, pages_per_block):
                ksum = ksum + page_refs[r * pages_per_block + s][0]
            per_key = jnp.sum(ksum * qcol, axis=1)
            gate = jnp.sum(per_key, axis=1, keepdims=True) * (1.0 / (page_rows * pages_per_block))
            g = jnp.where(lane == chunk * blocks + r, gate, g)
        g_ref[...] = g
        if k < n_jobs - 1:
            return
        col = lane.astype(F32)
        out = jnp.zeros((n_heads, LANES), F32)
        for r in range(MOBA_TOPK):
            mx = jnp.max(g, axis=1, keepdims=True)
            idx = jnp.min(jnp.where(g == mx, col, float(LANES)), axis=1, keepdims=True)
            out = jnp.where(lane == r, idx, out)
            g = jnp.where(col == idx, -jnp.inf, g)
        idx_ref[0] = jnp.minimum(out.astype(jnp.int32), nblk - 1)

    return [functools.partial(job, k) for k in range(n_jobs)]


def _tail_prompt_kernel(pt_ref, x_ref, yt_ref, u_ref, halo_ref, sg_ref, p_ref, q_ref, *rest, n_page_refs,
                        tiles_per_seq, chunks_per_sample, pages_per_block, nblk_sample, alpha, ff_chunk):
    page_refs, rest = rest[:n_page_refs], rest[n_page_refs:]
    *w_refs, o_ref, idx_ref, yb_ref, ue_ref, g_ref = rest
    blocks = n_page_refs // pages_per_block
    n_jobs = max(j for j in (1, 2, 4) if blocks % j == 0)
    side_work = _rank_sample_jobs(page_refs, q_ref, g_ref, idx_ref, pl.program_id(0) % chunks_per_sample,
                                  pages_per_block, nblk_sample, n_jobs)

    tm = x_ref.shape[0]
    ti = pl.program_id(0) % tiles_per_seq
    u = u_ref[...]
    ue_ref[POOL_HALO:POOL_HALO + tm, :] = u
    ue_ref[0:POOL_HALO, :] = jnp.where(ti == 0, 0.0, halo_ref[...])

    pool_ch = u.shape[1] // len(POOL_WINDOWS)
    pos = ti * tm + lax.broadcasted_iota(jnp.int32, (tm, pool_ch), 0)
    d = []
    for g, w in enumerate(POOL_WINDOWS):
        sl = slice(g * pool_ch, (g + 1) * pool_ch)
        pooled = u[:, sl]
        for s in range(1, w):
            pooled = pooled + ue_ref[POOL_HALO - s:POOL_HALO - s + tm, sl]
        cnt = jnp.minimum(w, pos + 1).astype(F32)
        d.append(pooled / cnt - u[:, sl])
    ya_b = yt_ref[0].T.astype(BF16)
    _tail(x_ref[...], ya_b, d, sg_ref, p_ref, *w_refs, o_ref, yb_ref, alpha=alpha, ff_chunk=ff_chunk,
          side_work=side_work)


def _tail_sample_kernel(x_ref, ya_ref, u_ref, st_ref, sg_ref, p_ref, *rest, cnts, alpha, ff_chunk):
    *w_refs, o_ref, yb_ref = rest
    u = u_ref[...]
    d_pool = u.shape[1]
    pool_ch = d_pool // len(POOL_WINDOWS)
    d = []
    for g, w in enumerate(POOL_WINDOWS):
        pooled = u[:, g * pool_ch:(g + 1) * pool_ch]
        for s in range(1, w):
            pooled = pooled + st_ref[POOL_BUF - s][:, g * pool_ch:(g + 1) * pool_ch]
        d.append(pooled / cnts[g] - u[:, g * pool_ch:(g + 1) * pool_ch])
    _tail(x_ref[...], ya_ref[...].astype(BF16), d, sg_ref, p_ref, *w_refs, o_ref, yb_ref,
          alpha=alpha, ff_chunk=ff_chunk)


def _weight_specs(weights, index_map_factory):
    return [_resident(w.shape, index_map_factory(w.ndim)) for w in weights]


def _tail_prompt(x2d, yt, u, sg, p2d, weights, seq, alpha, page_table, qcol, cache_kt, tm=256, ff_chunk=1024):
    t, d_model = x2d.shape
    d_pool = u.shape[1]
    batch = t // seq
    nt = seq // tm
    hb = tm // POOL_HALO
    steps = t // tm
    n_dec, n_pages = page_table.shape
    n_heads, head_dim, page_rows = cache_kt.shape[1:]
    ppb = MOBA_BLOCK // page_rows
    nblk = n_pages // ppb
    assert steps % n_dec == 0 and MOBA_TOPK <= nblk <= LANES
    cps = steps // n_dec
    assert n_pages % (cps * ppb) == 0
    pps = n_pages // cps
    yt3 = yt.reshape(batch, yt.shape[1] * yt.shape[2], seq)
    pages_by_step = page_table.reshape(steps, pps)
    page_specs = [pl.BlockSpec((1, n_heads, head_dim, page_rows), lambda i, pt, r=r: (pt[i, r], 0, 0, 0))
                  for r in range(pps)]
    in_specs = [
        pl.BlockSpec((tm, d_model), lambda i, pt: (i, 0)),
        pl.BlockSpec((1, yt3.shape[1], tm), lambda i, pt: (i // nt, 0, i % nt)),
        pl.BlockSpec((tm, d_pool), lambda i, pt: (i, 0)),
        pl.BlockSpec((POOL_HALO, d_pool), lambda i, pt: (jnp.maximum(i * hb - 1, 0), 0)),
        pl.BlockSpec((tm, sg.shape[1]), lambda i, pt: (i, 0)),
        pl.BlockSpec((tm, p2d.shape[1]), lambda i, pt: (i, 0)),
        pl.BlockSpec((1, n_heads * head_dim, 1), lambda i, pt: (i // cps, 0, 0)),
    ] + page_specs + _weight_specs(weights, lambda nd: (lambda i, pt: (0,) * nd))
    grid_spec = pltpu.PrefetchScalarGridSpec(
        num_scalar_prefetch=1,
        grid=(steps,),
        in_specs=in_specs,
        out_specs=(pl.BlockSpec((tm, d_model), lambda i, pt: (i, 0)),
                   pl.BlockSpec((1, n_heads, LANES), lambda i, pt: (i // cps, 0, 0))),
        scratch_shapes=[pltpu.VMEM((tm, d_pool), BF16), pltpu.VMEM((POOL_HALO + tm, d_pool), F32),
                        pltpu.VMEM((n_heads, LANES), F32)],
    )
    return pl.pallas_call(
        functools.partial(_tail_prompt_kernel, n_page_refs=pps, tiles_per_seq=nt, chunks_per_sample=cps,
                          pages_per_block=ppb, nblk_sample=nblk, alpha=alpha, ff_chunk=ff_chunk),
        out_shape=(jax.ShapeDtypeStruct((t, d_model), F32),
                   jax.ShapeDtypeStruct((n_dec, n_heads, LANES), jnp.int32)),
        grid_spec=grid_spec,
        compiler_params=_params(("arbitrary",)),
        name="tail_prompt",
    )(pages_by_step, x2d, yt3, u, u, sg, p2d, qcol, *([cache_kt] * pps), *weights)


def _tail_sample(x2d, ya, u, state_t, sg, p2d, weights, past_len, alpha, ff_chunk=1024):
    n, d_model = x2d.shape
    d_pool = u.shape[1]
    cnts = tuple(float(min(w, past_len + 1)) for w in POOL_WINDOWS)
    operands = (x2d, ya, u, state_t, sg, p2d)
    whole = lambda nd: (lambda i: (0,) * nd)
    in_specs = [pl.BlockSpec(a.shape, whole(a.ndim)) for a in operands] + _weight_specs(weights, whole)
    return pl.pallas_call(
        functools.partial(_tail_sample_kernel, cnts=cnts, alpha=alpha, ff_chunk=ff_chunk),
        out_shape=jax.ShapeDtypeStruct((n, d_model), F32),
        grid=(1,),
        in_specs=in_specs,
        out_specs=pl.BlockSpec((n, d_model), lambda i: (0, 0)),
        scratch_shapes=[pltpu.VMEM((n, d_pool), BF16)],
        compiler_params=_params(("arbitrary",)),
        name="tail_sample",
    )(*operands, *weights)


def _proj_sample_kernel(x_ref, w_ref, wq_ref, q_ref, k_ref, v_ref, u_ref, sg_ref, qg_ref, *, d_attn, d_pool):
    x = x_ref[...]
    z = _dot(x.astype(BF16), w_ref[...])
    q_ref[...] = z[:, 0:d_attn] * (HEAD_DIM ** -0.5)
    k_ref[...] = z[:, d_attn:2 * d_attn]
    v_ref[...] = z[:, 2 * d_attn:3 * d_attn]
    u_ref[...] = z[:, 3 * d_attn:3 * d_attn + d_pool]
    sg_ref[...] = _sigmoid(z[:, 3 * d_attn + d_pool:]).astype(BF16)
    qg_ref[...] = jnp.dot(x, wq_ref[...], precision=lax.Precision.HIGHEST, preferred_element_type=F32)


def _proj_sample(x2d, w_in, wq_f32):
    n, d_model = x2d.shape
    d_attn = d_model // 2
    d_pool = d_model // 2
    n_gate = w_in.shape[1] - 3 * d_attn - d_pool
    shapes = [(n, d_attn)] * 3 + [(n, d_pool)]
    whole = lambda s: pl.BlockSpec(s, lambda i: (0, 0))
    return pl.pallas_call(
        functools.partial(_proj_sample_kernel, d_attn=d_attn, d_pool=d_pool),
        out_shape=tuple(jax.ShapeDtypeStruct(s, F32) for s in shapes)
        + (jax.ShapeDtypeStruct((n, n_gate), BF16), jax.ShapeDtypeStruct((n, d_attn), F32)),
        grid=(1,),
        in_specs=[whole(x2d.shape), whole(w_in.shape), whole(wq_f32.shape)],
        out_specs=tuple(whole(s) for s in shapes) + (whole((n, n_gate)), whole((n, d_attn))),
        compiler_params=_params(("arbitrary",)),
        name="proj_sample",
    )(x2d, w_in, wq_f32)


def _attn_sample_kernel(pt_ref, sel_ref, slope_ref, q_ref, kn_ref, vn_ref, ck_ref, cv_ref, o_ref,
                        kbuf, vbuf, sem, *, pages_per_block, past_len):
    b = pl.program_id(0)
    n_seq = pl.num_programs(0)
    n_heads = q_ref.shape[1]
    head_dim, page_rows = kbuf.shape[2:]
    tiles_per_head = MOBA_TOPK * pages_per_block
    slot = b & 1

    def tile_copies(seq, slot_, h, t):
        r, s = divmod(t, pages_per_block)
        blk = sel_ref[(seq * n_heads + h) * MOBA_TOPK + r]
        page = pt_ref[seq, blk * pages_per_block + s]
        i = h * tiles_per_head + t
        return (pltpu.make_async_copy(ck_ref.at[page, h], kbuf.at[slot_, i], sem.at[slot_]),
                pltpu.make_async_copy(cv_ref.at[page, h], vbuf.at[slot_, i], sem.at[slot_]))

    def start_gather(seq, slot_):
        for h in range(n_heads):
            for t in range(tiles_per_head):
                for copy in tile_copies(seq, slot_, h, t):
                    copy.start()

    @pl.when(b == 0)
    def _():
        start_gather(0, 0)

    @pl.when(b + 1 < n_seq)
    def _():
        start_gather(b + 1, 1 - slot)

    for h in range(n_heads):
        for t in range(tiles_per_head):
            for copy in tile_copies(b, slot, h, t):
                copy.wait()

    kpos = lax.broadcasted_iota(jnp.int32, (1, page_rows), 1)
    for h in range(n_heads):
        q = q_ref[0, h].astype(BF16)
        q8 = jnp.broadcast_to(q, (8, head_dim))
        slope = slope_ref[h][:, 0:page_rows]
        s_self = jnp.sum(q.astype(F32) * kn_ref[0, h].astype(BF16).astype(F32), axis=1, keepdims=True)
        scores = []
        m = s_self
        for t in range(tiles_per_head):
            r, s = divmod(t, pages_per_block)
            blk = sel_ref[(b * n_heads + h) * MOBA_TOPK + r]
            kt = kbuf[slot, h * tiles_per_head + t].astype(BF16)
            dist = past_len - (blk * MOBA_BLOCK + s * page_rows + kpos)
            sc = _dot(q8, kt)[0:1, :] - slope * dist.astype(F32)
            scores.append(sc)
            m = jnp.maximum(m, jnp.max(sc, axis=1, keepdims=True))
        p_self = jnp.exp(s_self - m)
        l = p_self
        acc = p_self * vn_ref[0, h]
        for t, sc in enumerate(scores):
            p = jnp.exp(sc - m)
            l = l + jnp.sum(p, axis=1, keepdims=True)
            p8 = jnp.broadcast_to(p.astype(BF16), (8, page_rows))
            acc = acc + _dot_nt(p8, vbuf[slot, h * tiles_per_head + t].astype(BF16))[0:1, :]
        o_ref[0, h] = acc / l


def _attn_sample(page_table, sel, slopes, q4, k4, v4, cache_kt, cache_vt, past_len):
    n, n_heads, _, head_dim = q4.shape
    page_rows = cache_kt.shape[3]
    ppb = MOBA_BLOCK // page_rows
    n_tiles = n_heads * MOBA_TOPK * ppb
    tok = pl.BlockSpec((1, n_heads, 1, head_dim), lambda b, pt, sl: (b, 0, 0, 0))
    grid_spec = pltpu.PrefetchScalarGridSpec(
        num_scalar_prefetch=2,
        grid=(n,),
        in_specs=[pl.BlockSpec(slopes.shape, lambda b, pt, sl: (0, 0, 0)), tok, tok, tok,
                  pl.BlockSpec(memory_space=pl.ANY), pl.BlockSpec(memory_space=pl.ANY)],
        out_specs=tok,
        scratch_shapes=[pltpu.VMEM((2, n_tiles, head_dim, page_rows), F32),
                        pltpu.VMEM((2, n_tiles, head_dim, page_rows), F32),
                        pltpu.SemaphoreType.DMA((2,))],
    )
    return pl.pallas_call(
        functools.partial(_attn_sample_kernel, pages_per_block=ppb, past_len=past_len),
        out_shape=jax.ShapeDtypeStruct(q4.shape, F32),
        grid_spec=grid_spec,
        compiler_params=_params(("arbitrary",)),
        name="attn_sample",
    )(page_table, sel, slopes, q4, k4, v4, cache_kt, cache_vt)


def kernel(x_prompt, x_sample, cache_k, cache_v, state_pool, page_table, p_prompt, p_sample, w_in, w_branch,
           w_out, ln1_g, ln1_b, w_up, w_down, ln2_g, ln2_b, w_pe, w_pg, w_pool, pool_scale):
    depth = w_in.shape[0]
    batch, seq, d_model = x_prompt.shape
    n_dec, dec_seq, _ = x_sample.shape
    assert dec_seq == 1, "the sample path handles one new token per sequence"
    d_attn = d_model // 2
    d_pool = d_model // 2
    n_heads = d_attn // HEAD_DIM
    n_phys, page_rows = cache_k.shape[1:3]
    past_len = page_table.shape[1] * page_rows
    alpha = (2.0 * depth) ** 0.25
    nblk = seq // MOBA_BLOCK
    nrow = -(-nblk // 16) * 16
    assert past_len // MOBA_BLOCK >= MOBA_TOPK and page_rows <= LANES
    slopes = jnp.asarray([[[_alibi_slope(h, n_heads)] * MOBA_BLOCK] for h in range(n_heads)], F32)

    y_p = x_prompt.reshape(batch * seq, d_model)
    y_s = x_sample.reshape(n_dec, d_model)
    outs = [[] for _ in range(6)]
    for i in range(depth):
        w_in_b = w_in[i].astype(BF16)
        row = lambda a: a[i].reshape(1, -1)
        weights = (w_pool[i].astype(BF16), row(pool_scale), w_branch[i].astype(BF16), w_out[i].astype(BF16),
                   row(ln1_g), row(ln1_b), w_up[i].astype(BF16), w_down[i].astype(BF16),
                   row(ln2_g), row(ln2_b), w_pe[i].astype(BF16), w_pg[i].astype(BF16))

        qs, ks, vs, us, sgs, qg = _proj_sample(y_s, w_in_b, w_in[i][:, :d_attn])
        ckt = cache_k[i].transpose(0, 2, 3, 1)
        cvt = cache_v[i].transpose(0, 2, 3, 1)

        kt, vtf, u, sg, qt, ka, vt, km = _proj_prompt(y_p, w_in_b, batch, seq)
        km = km.reshape(batch, nblk, n_heads, HEAD_DIM).transpose(0, 2, 1, 3)
        km = jnp.pad(km, ((0, 0), (0, 0), (0, nrow - nblk), (0, HEAD_DIM))).astype(BF16)
        yt = _attn_prompt(slopes * LOG2E, qt, ka, vt, km)
        y_p, sel = _tail_prompt(y_p, yt, u, sg, p_prompt[i].reshape(batch * seq, -1), weights, seq, alpha,
                                page_table, qg.reshape(n_dec, d_attn, 1), ckt)
        outs[0].append(kt.transpose(0, 3, 1, 2))
        outs[1].append(vtf.transpose(0, 3, 1, 2))
        outs[2].append(u.reshape(batch, seq, d_pool)[:, seq - POOL_BUF:])

        q4, k4, v4 = (a.reshape(n_dec, n_heads, 1, HEAD_DIM) for a in (qs, ks, vs))
        sel = sel[:, :, :MOBA_TOPK].reshape(-1)
        ya_s = _attn_sample(page_table, sel, slopes, q4, k4, v4, ckt, cvt, past_len).reshape(n_dec, d_attn)
        state = state_pool[i]
        y_s = _tail_sample(y_s, ya_s, us, state.transpose(1, 0, 2), sgs,
                           p_sample[i].reshape(n_dec, -1), weights, past_len, alpha)
        outs[3].append(ks.reshape(n_dec, 1, n_heads, HEAD_DIM))
        outs[4].append(vs.reshape(n_dec, 1, n_heads, HEAD_DIM))
        outs[5].append(jnp.concatenate([state[:, 1:], us[:, None, :]], axis=1))

    k_p, v_p, pool_p, k_s, v_s, pool_s = (jnp.stack(o) for o in outs)
    return (y_p.reshape(batch, seq, d_model), y_s.reshape(n_dec, 1, d_model), k_p, v_p, pool_p, k_s, v_s, pool_s)
```

```python
import functools

import jax
import jax.numpy as jnp
import numpy as np
from jax import lax
from jax.experimental import pallas as pl
from jax.experimental.pallas import tpu as pltpu

HEAD_DIM = 64
MOBA_BLOCK = 256
MOBA_TOPK = 3
POOL_WINDOWS = (2, 4, 8, 16)
POOL_BUF = max(POOL_WINDOWS) - 1
POOL_HALO = 16
LN_EPS = 1e-5
NEG_INF = -1e30
LANES = 128
VMEM_LIMIT_BYTES = 56 * 1024 * 1024

F32 = jnp.float32
BF16 = jnp.bfloat16


def _dot(a, b):
    return jnp.dot(a, b, preferred_element_type=F32)


def _dot_nt(a, b):
    return lax.dot_general(a, b, (((1,), (1,)), ((), ())), preferred_element_type=F32)


def _resident(block_shape, index_map):
    return pl.BlockSpec(block_shape, index_map, pipeline_mode=pl.Buffered(1))


def _params(semantics, flags=None):
    return pltpu.CompilerParams(dimension_semantics=semantics, vmem_limit_bytes=VMEM_LIMIT_BYTES, flags=flags)


def _sigmoid(x):
    return 1.0 / (1.0 + jnp.exp(-x))


def _layer_norm(x, g, b):
    mu = jnp.mean(x, axis=-1, keepdims=True)
    xc = x - mu
    var = jnp.mean(xc * xc, axis=-1, keepdims=True)
    return xc * lax.rsqrt(var + LN_EPS) * g + b


LOG2E = 1.4426950408889634
V_PAD = 16


def _alibi_slope(h, n_heads):
    return 2.0 ** (-8.0 * (h + 1) / n_heads)


def _split_bf16(c):
    hi = float(np.asarray(c, dtype=BF16))
    lo = float(np.asarray(c - hi, dtype=BF16))
    return hi, lo


def _proj_prompt_kernel(x_ref, w_ref, k_ref, v_ref, u_ref, sg_ref, qt_ref, ka_ref, vt_ref, km_ref,
                        *, d_attn, d_pool, n_heads):
    tm = x_ref.shape[0]
    xb = x_ref[...].astype(BF16)
    o_k, o_v, o_u, o_g = d_attn, 2 * d_attn, 3 * d_attn, 3 * d_attn + d_pool

    q = _dot(xb, w_ref[:, 0:o_k]) * (LOG2E * HEAD_DIM ** -0.5)
    qt = q.T
    row = lax.broadcasted_iota(jnp.int32, (HEAD_DIM, tm), 0)
    for h in range(n_heads):
        c_hi, c_lo = _split_bf16(LOG2E * _alibi_slope(h, n_heads))
        qt_ref[0, h, 0:HEAD_DIM, :] = qt[h * HEAD_DIM:(h + 1) * HEAD_DIM, :].astype(BF16)
        extra = jnp.where(row == 0, 16.0 * c_hi, jnp.where(row == 1, c_hi,
                          jnp.where(row == 2, 16.0 * c_lo, jnp.where(row == 3, c_lo, 0.0))))
        qt_ref[0, h, HEAD_DIM:2 * HEAD_DIM, :] = extra.astype(BF16)

    k = _dot(xb, w_ref[:, o_k:o_v])
    kt = k.T
    for h in range(n_heads):
        k_ref[0, h] = kt[h * HEAD_DIM:(h + 1) * HEAD_DIM, :]
    for c in range(tm // MOBA_BLOCK):
        blk = k[c * MOBA_BLOCK:(c + 1) * MOBA_BLOCK, :]
        km_ref[c] = jnp.sum(blk, axis=0, keepdims=True) * (1.0 / MOBA_BLOCK)
    prow = lax.broadcasted_iota(jnp.int32, (tm, LANES), 0) & (MOBA_BLOCK - 1)
    lane = lax.broadcasted_iota(jnp.int32, (tm, LANES), 1)
    sixteens = jnp.logical_or(lane == HEAD_DIM, lane == HEAD_DIM + 2)
    units = jnp.logical_or(lane == HEAD_DIM + 1, lane == HEAD_DIM + 3)
    posf = jnp.where(sixteens, (prow >> 4).astype(F32), jnp.where(units, (prow & 15).astype(F32), 0.0))
    low = lane < HEAD_DIM
    for m in range(n_heads // 2):
        col = k[:, m * LANES:(m + 1) * LANES]
        ka_ref[0, 2 * m] = jnp.where(low, col, posf).astype(BF16)
        ka_ref[0, 2 * m + 1] = jnp.where(low, pltpu.roll(col, HEAD_DIM, 1), posf).astype(BF16)

    vt = _dot(xb, w_ref[:, o_v:o_u]).T
    ones_row = jnp.where(lax.broadcasted_iota(jnp.int32, (V_PAD, MOBA_BLOCK), 0) == 0, 1.0, 0.0).astype(BF16)
    for h in range(n_heads):
        v_ref[0, h] = vt[h * HEAD_DIM:(h + 1) * HEAD_DIM, :]
        for c in range(tm // MOBA_BLOCK):
            vt_ref[0, h, c, 0:HEAD_DIM, :] = vt[h * HEAD_DIM:(h + 1) * HEAD_DIM,
                                                c * MOBA_BLOCK:(c + 1) * MOBA_BLOCK].astype(BF16)
            vt_ref[0, h, c, HEAD_DIM:HEAD_DIM + V_PAD, :] = ones_row

    u_ref[...] = _dot(xb, w_ref[:, o_u:o_g])
    sg_ref[...] = _sigmoid(_dot(xb, w_ref[:, o_g:])).astype(BF16)


def _proj_prompt(x2d, w_in, batch, seq, tm=512):
    t, d_model = x2d.shape
    d_attn = d_model // 2
    d_pool = d_model // 2
    n_heads = d_attn // HEAD_DIM
    d_in = w_in.shape[1]
    n_gate = d_in - 3 * d_attn - d_pool
    nt = seq // tm
    nblk = seq // MOBA_BLOCK
    cb = tm // MOBA_BLOCK
    out_shape = (
        jax.ShapeDtypeStruct((batch, n_heads, HEAD_DIM, seq), F32),
        jax.ShapeDtypeStruct((batch, n_heads, HEAD_DIM, seq), F32),
        jax.ShapeDtypeStruct((t, d_pool), F32),
        jax.ShapeDtypeStruct((t, n_gate), BF16),
        jax.ShapeDtypeStruct((batch, n_heads, 2 * HEAD_DIM, seq), BF16),
        jax.ShapeDtypeStruct((batch, n_heads, seq, 2 * HEAD_DIM), BF16),
        jax.ShapeDtypeStruct((batch, n_heads, nblk, HEAD_DIM + V_PAD, MOBA_BLOCK), BF16),
        jax.ShapeDtypeStruct((t // MOBA_BLOCK, 1, d_attn), F32),
    )
    out_specs = (
        pl.BlockSpec((1, n_heads, HEAD_DIM, tm), lambda i: (i // nt, 0, 0, i % nt)),
        pl.BlockSpec((1, n_heads, HEAD_DIM, tm), lambda i: (i // nt, 0, 0, i % nt)),
        pl.BlockSpec((tm, d_pool), lambda i: (i, 0)),
        pl.BlockSpec((tm, n_gate), lambda i: (i, 0)),
        pl.BlockSpec((1, n_heads, 2 * HEAD_DIM, tm), lambda i: (i // nt, 0, 0, i % nt)),
        pl.BlockSpec((1, n_heads, tm, 2 * HEAD_DIM), lambda i: (i // nt, 0, i % nt, 0)),
        pl.BlockSpec((1, n_heads, cb, HEAD_DIM + V_PAD, MOBA_BLOCK), lambda i: (i // nt, 0, i % nt, 0, 0)),
        pl.BlockSpec((cb, 1, d_attn), lambda i: (i, 0, 0)),
    )
    return pl.pallas_call(
        functools.partial(_proj_prompt_kernel, d_attn=d_attn, d_pool=d_pool, n_heads=n_heads),
        out_shape=out_shape,
        grid=(t // tm,),
        in_specs=[pl.BlockSpec((tm, d_model), lambda i: (i, 0)),
                  _resident((d_model, d_in), lambda i: (0, 0))],
        out_specs=out_specs,
        compiler_params=_params(("arbitrary",)),
        name="proj_prompt",
    )(x2d, w_in)


def _attn_prompt_kernel(slope_ref, qt_ref, ka_ref, vt_ref, km_ref, o_ref, mask_ref, m_ref, acc_ref, s_ref):
    qi = pl.program_id(1)
    n_heads, nrow, tq = mask_ref.shape
    own = 2
    row = lax.broadcasted_iota(jnp.int32, (nrow, tq), 0).astype(F32)
    qif = jnp.full((1, tq), qi, jnp.int32).astype(F32)
    ik = lax.broadcasted_iota(jnp.int32, (MOBA_BLOCK, tq), 0)
    iq = lax.broadcasted_iota(jnp.int32, (MOBA_BLOCK, tq), 1)

    def scores(j, slot, heads=range(n_heads)):
        off = pl.multiple_of(j * MOBA_BLOCK, MOBA_BLOCK)
        for h in heads:
            s_ref[slot, h] = _dot(ka_ref[0, h, pl.ds(off, MOBA_BLOCK), :], qt_ref[0, h])

    def absorb(j, slot, heads=range(n_heads)):
        for h in heads:
            ps, alphas = [], []
            for c in range(tq // LANES):
                cols = slice(c * LANES, (c + 1) * LANES)
                s = s_ref[slot, h, :, cols]
                bias = mask_ref[h, pl.ds(j, 1), :][:, cols]
                m = m_ref[h, :, cols]
                m_new = jnp.maximum(m, jnp.max(s, axis=0, keepdims=True) + bias)
                alphas.append(jnp.exp2(m - m_new))
                ps.append(jnp.exp2(s - (m_new - bias)).astype(BF16))
                m_ref[h, :, cols] = m_new
            p = jnp.concatenate(ps, axis=1)
            acc_ref[h] = jnp.concatenate(alphas, axis=1) * acc_ref[h] + _dot(vt_ref[0, h, j], p)

    gates = [_dot(km_ref[0, h], qt_ref[0, h]) for h in range(n_heads)]
    for h in range(n_heads):
        scores(qi, own, [h])
        g = jnp.where(row < qif, gates[h], NEG_INF)
        sel = jnp.zeros((nrow, tq), jnp.bool_)
        for r in range(MOBA_TOPK):
            mx = jnp.max(g, axis=0, keepdims=True)
            idx = jnp.min(jnp.where(g == mx, row, float(nrow)), axis=0, keepdims=True)
            hit = row == idx
            sel = jnp.logical_or(sel, jnp.logical_and(hit, qif > float(r)))
            g = jnp.where(hit, -jnp.inf, g)
        mask_ref[h] = jnp.where(sel, slope_ref[h] * float(MOBA_BLOCK) * (row - qif), NEG_INF)

    for h in range(n_heads):
        scores(0, 0, [h])
        s = jnp.where(ik <= iq, s_ref[own, h], NEG_INF)
        m0 = jnp.max(s, axis=0, keepdims=True)
        p = jnp.exp2(s - m0)
        m_ref[h] = m0
        acc_ref[h] = _dot(vt_ref[0, h, qi], p.astype(BF16))

    for h in range(n_heads):
        scores(1, 1, [h])
        absorb(0, 0, [h])

    def body(jj, carry):
        j = 2 * jj
        for h in range(n_heads):
            scores(j, 0, [h])
            absorb(j - 1, 1, [h])
        for h in range(n_heads):
            scores(j + 1, 1, [h])
            absorb(j, 0, [h])
        return carry

    pairs = jnp.maximum((qi + 1) >> 1, 1)
    lax.fori_loop(1, pairs, body, 0)
    last = 2 * pairs - 1
    absorb(last, 1)

    for h in range(n_heads):
        o_ref[0, h] = acc_ref[h, 0:HEAD_DIM, :] * (1.0 / acc_ref[h, HEAD_DIM:HEAD_DIM + 1, :])


def _attn_prompt(slopes, qt, ka, vt, km):
    batch, n_heads, _, seq = qt.shape
    nblk = seq // MOBA_BLOCK
    nrow = km.shape[2]
    per_batch = lambda nd: (lambda b, i: (b,) + (0,) * (nd - 1))
    return pl.pallas_call(
        _attn_prompt_kernel,
        out_shape=jax.ShapeDtypeStruct((batch, n_heads, HEAD_DIM, seq), F32),
        grid=(batch, nblk),
        in_specs=[
            pl.BlockSpec(slopes.shape, lambda b, i: (0, 0, 0)),
            pl.BlockSpec((1, n_heads, 2 * HEAD_DIM, MOBA_BLOCK), lambda b, i: (b, 0, 0, i)),
            _resident((1,) + ka.shape[1:], per_batch(4)),
            _resident((1,) + vt.shape[1:], per_batch(5)),
            pl.BlockSpec((1,) + km.shape[1:], per_batch(4)),
        ],
        out_specs=pl.BlockSpec((1, n_heads, HEAD_DIM, MOBA_BLOCK), lambda b, i: (b, 0, 0, i)),
        scratch_shapes=[pltpu.VMEM((n_heads, nrow, MOBA_BLOCK), F32),
                        pltpu.VMEM((n_heads, 1, MOBA_BLOCK), F32),
                        pltpu.VMEM((n_heads, HEAD_DIM + V_PAD, MOBA_BLOCK), F32),
                        pltpu.VMEM((3, n_heads, MOBA_BLOCK, MOBA_BLOCK), F32)],
        compiler_params=_params(("arbitrary", "arbitrary")),
        name="attn_prompt",
    )(slopes, qt, ka, vt, km)


def _tail(x, ya_b, d, sg_ref, p_ref, wp_ref, ps_ref, wb_ref, wo_ref, g1_ref, b1_ref, wu_ref, wd_ref,
          g2_ref, b2_ref, wpe_ref, wpg_ref, o_ref, yb_ref, *, alpha, ff_chunk, side_work=()):
    d_model = x.shape[1]
    pool_ch = wp_ref.shape[1]
    for g in range(len(POOL_WINDOWS)):
        sl = slice(g * pool_ch, (g + 1) * pool_ch)
        yb = _dot(d[g].astype(BF16), wp_ref[g]) * ps_ref[:, sl]
        yb_ref[:, sl] = yb.astype(BF16)
    br0 = _dot(ya_b, wb_ref[0])
    br1 = _dot(yb_ref[...], wb_ref[1])
    merged = sg_ref[:, 0:d_model].astype(F32) * br0 + sg_ref[:, d_model:].astype(F32) * br1
    mix = _dot(merged.astype(BF16), wo_ref[...])
    pe = _dot(p_ref[...].astype(BF16), wpe_ref[...])
    h1 = _layer_norm(alpha * x + mix, g1_ref[...], b1_ref[...])
    h1b = h1.astype(BF16)
    ple = _sigmoid(_dot(h1b, wpg_ref[...])) * pe
    acc = alpha * h1 + ple
    d_ff = wu_ref.shape[1]
    assert len(side_work) <= d_ff // ff_chunk
    for c in range(d_ff // ff_chunk):
        if c < len(side_work):
            side_work[c]()
        a = jnp.maximum(_dot(h1b, wu_ref[:, c * ff_chunk:(c + 1) * ff_chunk]), 0.0)
        acc = acc + _dot((a * a).astype(BF16), wd_ref[c * ff_chunk:(c + 1) * ff_chunk, :])
    o_ref[...] = _layer_norm(acc, g2_ref[...], b2_ref[...])


def _rank_sample_jobs(page_refs, q_ref, g_ref, idx_ref, chunk, pages_per_block, nblk, n_jobs):
    n_heads, head_dim, page_rows = page_refs[0].shape[1:]
    blocks = len(page_refs) // pages_per_block
    assert blocks % n_jobs == 0
    per_job = blocks // n_jobs

    def job(k):
        lane = lax.broadcasted_iota(jnp.int32, (n_heads, LANES), 1)
        qcol = jnp.broadcast_to(q_ref[0], (n_heads * head_dim, page_rows)).reshape(n_heads, head_dim, page_rows)
        g = g_ref[...]
        if k == 0:
            g = jnp.where(chunk == 0, -jnp.inf, g)
        for r in range(k * per_job, (k + 1) * per_job):
            ksum = page_refs[r * pages_per_block][0]
            for s in range(1, pages_per_block):
                ksum = ksum + page_refs[r * pages_per_block + s][0]
            per_key = jnp.sum(ksum * qcol, axis=1)
            gate = jnp.sum(per_key, axis=1, keepdims=True) * (1.0 / (page_rows * pages_per_block))
            g = jnp.where(lane == chunk * blocks + r, gate, g)
        g_ref[...] = g
        if k < n_jobs - 1:
            return
        col = lane.astype(F32)
        out = jnp.zeros((n_heads, LANES), F32)
        for r in range(MOBA_TOPK):
            mx = jnp.max(g, axis=1, keepdims=True)
            idx = jnp.min(jnp.where(g == mx, col, float(LANES)), axis=1, keepdims=True)
            out = jnp.where(lane == r, idx, out)
            g = jnp.where(col == idx, -jnp.inf, g)
        idx_ref[0] = jnp.minimum(out.astype(jnp.int32), nblk - 1)

    return [functools.partial(job, k) for k in range(n_jobs)]


def _tail_prompt_kernel(pt_ref, x_ref, yt_ref, u_ref, halo_ref, sg_ref, p_ref, q_ref, *rest, n_page_refs,
                        tiles_per_seq, chunks_per_sample, pages_per_block, nblk_sample, alpha, ff_chunk):
    page_refs, rest = rest[:n_page_refs], rest[n_page_refs:]
    *w_refs, o_ref, idx_ref, yb_ref, ue_ref, g_ref = rest
    blocks = n_page_refs // pages_per_block
    n_mlp_chunks = w_refs[6].shape[1] // ff_chunk
    n_jobs = max(j for j in range(1, n_mlp_chunks + 1) if blocks % j == 0)
    side_work = _rank_sample_jobs(page_refs, q_ref, g_ref, idx_ref, pl.program_id(0) % chunks_per_sample,
                                  pages_per_block, nblk_sample, n_jobs)

    tm = x_ref.shape[0]
    ti = pl.program_id(0) % tiles_per_seq
    u = u_ref[...]
    ue_ref[POOL_HALO:POOL_HALO + tm, :] = u
    ue_ref[0:POOL_HALO, :] = jnp.where(ti == 0, 0.0, halo_ref[...])

    pool_ch = u.shape[1] // len(POOL_WINDOWS)
    pos = ti * tm + lax.broadcasted_iota(jnp.int32, (tm, pool_ch), 0)
    d = []
    for g, w in enumerate(POOL_WINDOWS):
        sl = slice(g * pool_ch, (g + 1) * pool_ch)
        pooled = u[:, sl]
        for s in range(1, w):
            pooled = pooled + ue_ref[POOL_HALO - s:POOL_HALO - s + tm, sl]
        cnt = jnp.minimum(w, pos + 1).astype(F32)
        d.append(pooled / cnt - u[:, sl])
    ya_b = yt_ref[0].T.astype(BF16)
    _tail(x_ref[...], ya_b, d, sg_ref, p_ref, *w_refs, o_ref, yb_ref, alpha=alpha, ff_chunk=ff_chunk,
          side_work=side_work)


def _tail_sample_kernel(x_ref, ya_ref, u_ref, st_ref, sg_ref, p_ref, *rest, cnts, alpha, ff_chunk):
    *w_refs, o_ref, yb_ref = rest
    u = u_ref[...]
    d_pool = u.shape[1]
    pool_ch = d_pool // len(POOL_WINDOWS)
    d = []
    for g, w in enumerate(POOL_WINDOWS):
        pooled = u[:, g * pool_ch:(g + 1) * pool_ch]
        for s in range(1, w):
            pooled = pooled + st_ref[POOL_BUF - s][:, g * pool_ch:(g + 1) * pool_ch]
        d.append(pooled / cnts[g] - u[:, g * pool_ch:(g + 1) * pool_ch])
    _tail(x_ref[...], ya_ref[...].astype(BF16), d, sg_ref, p_ref, *w_refs, o_ref, yb_ref,
          alpha=alpha, ff_chunk=ff_chunk)


def _weight_specs(weights, index_map_factory):
    return [_resident(w.shape, index_map_factory(w.ndim)) for w in weights]


def _tail_prompt(x2d, yt, u, sg, p2d, weights, seq, alpha, page_table, qcol, cache_kt, tm=256, ff_chunk=1024):
    t, d_model = x2d.shape
    d_pool = u.shape[1]
    batch = t // seq
    nt = seq // tm
    hb = tm // POOL_HALO
    steps = t // tm
    n_dec, n_pages = page_table.shape
    n_heads, head_dim, page_rows = cache_kt.shape[1:]
    ppb = MOBA_BLOCK // page_rows
    nblk = n_pages // ppb
    assert steps % n_dec == 0 and MOBA_TOPK <= nblk <= LANES
    cps = steps // n_dec
    assert n_pages % (cps * ppb) == 0
    pps = n_pages // cps
    yt3 = yt.reshape(batch, yt.shape[1] * yt.shape[2], seq)
    pages_by_step = page_table.reshape(steps, pps)
    page_specs = [pl.BlockSpec((1, n_heads, head_dim, page_rows), lambda i, pt, r=r: (pt[i, r], 0, 0, 0))
                  for r in range(pps)]
    in_specs = [
        pl.BlockSpec((tm, d_model), lambda i, pt: (i, 0)),
        pl.BlockSpec((1, yt3.shape[1], tm), lambda i, pt: (i // nt, 0, i % nt)),
        pl.BlockSpec((tm, d_pool), lambda i, pt: (i, 0)),
        pl.BlockSpec((POOL_HALO, d_pool), lambda i, pt: (jnp.maximum(i * hb - 1, 0), 0)),
        pl.BlockSpec((tm, sg.shape[1]), lambda i, pt: (i, 0)),
        pl.BlockSpec((tm, p2d.shape[1]), lambda i, pt: (i, 0)),
        pl.BlockSpec((1, n_heads * head_dim, 1), lambda i, pt: (i // cps, 0, 0)),
    ] + page_specs + _weight_specs(weights, lambda nd: (lambda i, pt: (0,) * nd))
    grid_spec = pltpu.PrefetchScalarGridSpec(
        num_scalar_prefetch=1,
        grid=(steps,),
        in_specs=in_specs,
        out_specs=(pl.BlockSpec((tm, d_model), lambda i, pt: (i, 0)),
                   pl.BlockSpec((1, n_heads, LANES), lambda i, pt: (i // cps, 0, 0))),
        scratch_shapes=[pltpu.VMEM((tm, d_pool), BF16), pltpu.VMEM((POOL_HALO + tm, d_pool), F32),
                        pltpu.VMEM((n_heads, LANES), F32)],
    )
    return pl.pallas_call(
        functools.partial(_tail_prompt_kernel, n_page_refs=pps, tiles_per_seq=nt, chunks_per_sample=cps,
                          pages_per_block=ppb, nblk_sample=nblk, alpha=alpha, ff_chunk=ff_chunk),
        out_shape=(jax.ShapeDtypeStruct((t, d_model), F32),
                   jax.ShapeDtypeStruct((n_dec, n_heads, LANES), jnp.int32)),
        grid_spec=grid_spec,
        compiler_params=_params(("arbitrary",)),
        name="tail_prompt",
    )(pages_by_step, x2d, yt3, u, u, sg, p2d, qcol, *([cache_kt] * pps), *weights)


def _tail_sample(x2d, ya, u, state_t, sg, p2d, weights, past_len, alpha, ff_chunk=1024):
    n, d_model = x2d.shape
    d_pool = u.shape[1]
    cnts = tuple(float(min(w, past_len + 1)) for w in POOL_WINDOWS)
    operands = (x2d, ya, u, state_t, sg, p2d)
    whole = lambda nd: (lambda i: (0,) * nd)
    in_specs = [pl.BlockSpec(a.shape, whole(a.ndim)) for a in operands] + _weight_specs(weights, whole)
    return pl.pallas_call(
        functools.partial(_tail_sample_kernel, cnts=cnts, alpha=alpha, ff_chunk=ff_chunk),
        out_shape=jax.ShapeDtypeStruct((n, d_model), F32),
        grid=(1,),
        in_specs=in_specs,
        out_specs=pl.BlockSpec((n, d_model), lambda i: (0, 0)),
        scratch_shapes=[pltpu.VMEM((n, d_pool), BF16)],
        compiler_params=_params(("arbitrary",)),
        name="tail_sample",
    )(*operands, *weights)


def _proj_sample_kernel(x_ref, w_ref, wq_ref, q_ref, k_ref, v_ref, u_ref, sg_ref, qg_ref, *, d_attn, d_pool):
    x = x_ref[...]
    z = _dot(x.astype(BF16), w_ref[...])
    q_ref[...] = z[:, 0:d_attn] * (HEAD_DIM ** -0.5)
    k_ref[...] = z[:, d_attn:2 * d_attn]
    v_ref[...] = z[:, 2 * d_attn:3 * d_attn]
    u_ref[...] = z[:, 3 * d_attn:3 * d_attn + d_pool]
    sg_ref[...] = _sigmoid(z[:, 3 * d_attn + d_pool:]).astype(BF16)
    qg_ref[...] = jnp.dot(x, wq_ref[...], precision=lax.Precision.HIGHEST, preferred_element_type=F32)


def _proj_sample(x2d, w_in, wq_f32):
    n, d_model = x2d.shape
    d_attn = d_model // 2
    d_pool = d_model // 2
    n_gate = w_in.shape[1] - 3 * d_attn - d_pool
    shapes = [(n, d_attn)] * 3 + [(n, d_pool)]
    whole = lambda s: pl.BlockSpec(s, lambda i: (0, 0))
    return pl.pallas_call(
        functools.partial(_proj_sample_kernel, d_attn=d_attn, d_pool=d_pool),
        out_shape=tuple(jax.ShapeDtypeStruct(s, F32) for s in shapes)
        + (jax.ShapeDtypeStruct((n, n_gate), BF16), jax.ShapeDtypeStruct((n, d_attn), F32)),
        grid=(1,),
        in_specs=[whole(x2d.shape), whole(w_in.shape), whole(wq_f32.shape)],
        out_specs=tuple(whole(s) for s in shapes) + (whole((n, n_gate)), whole((n, d_attn))),
        compiler_params=_params(("arbitrary",)),
        name="proj_sample",
    )(x2d, w_in, wq_f32)


def _attn_sample_kernel(pt_ref, sel_ref, slope_ref, q_ref, kn_ref, vn_ref, ck_ref, cv_ref, o_ref,
                        kbuf, vbuf, sem, *, pages_per_block, past_len):
    b = pl.program_id(0)
    n_seq = pl.num_programs(0)
    n_heads = q_ref.shape[1]
    head_dim, page_rows = kbuf.shape[2:]
    tiles_per_head = MOBA_TOPK * pages_per_block
    slot = b & 1

    def tile_copies(seq, slot_, h, t):
        r, s = divmod(t, pages_per_block)
        blk = sel_ref[(seq * n_heads + h) * MOBA_TOPK + r]
        page = pt_ref[seq, blk * pages_per_block + s]
        i = h * tiles_per_head + t
        return (pltpu.make_async_copy(ck_ref.at[page, h], kbuf.at[slot_, i], sem.at[slot_]),
                pltpu.make_async_copy(cv_ref.at[page, h], vbuf.at[slot_, i], sem.at[slot_]))

    def start_gather(seq, slot_):
        for h in range(n_heads):
            for t in range(tiles_per_head):
                for copy in tile_copies(seq, slot_, h, t):
                    copy.start()

    @pl.when(b == 0)
    def _():
        start_gather(0, 0)

    @pl.when(b + 1 < n_seq)
    def _():
        start_gather(b + 1, 1 - slot)

    for h in range(n_heads):
        for t in range(tiles_per_head):
            for copy in tile_copies(b, slot, h, t):
                copy.wait()

    kpos = lax.broadcasted_iota(jnp.int32, (1, page_rows), 1)
    for h in range(n_heads):
        q = q_ref[0, h].astype(BF16)
        q8 = jnp.broadcast_to(q, (8, head_dim))
        slope = slope_ref[h][:, 0:page_rows]
        s_self = jnp.sum(q.astype(F32) * kn_ref[0, h].astype(BF16).astype(F32), axis=1, keepdims=True)
        scores = []
        m = s_self
        for t in range(tiles_per_head):
            r, s = divmod(t, pages_per_block)
            blk = sel_ref[(b * n_heads + h) * MOBA_TOPK + r]
            kt = kbuf[slot, h * tiles_per_head + t].astype(BF16)
            dist = past_len - (blk * MOBA_BLOCK + s * page_rows + kpos)
            sc = _dot(q8, kt)[0:1, :] - slope * dist.astype(F32)
            scores.append(sc)
            m = jnp.maximum(m, jnp.max(sc, axis=1, keepdims=True))
        p_self = jnp.exp(s_self - m)
        l = p_self
        acc = p_self * vn_ref[0, h]
        for t, sc in enumerate(scores):
            p = jnp.exp(sc - m)
            l = l + jnp.sum(p, axis=1, keepdims=True)
            p8 = jnp.broadcast_to(p.astype(BF16), (8, page_rows))
            acc = acc + _dot_nt(p8, vbuf[slot, h * tiles_per_head + t].astype(BF16))[0:1, :]
        o_ref[0, h] = acc / l


def _attn_sample(page_table, sel, slopes, q4, k4, v4, cache_kt, cache_vt, past_len):
    n, n_heads, _, head_dim = q4.shape
    page_rows = cache_kt.shape[3]
    ppb = MOBA_BLOCK // page_rows
    n_tiles = n_heads * MOBA_TOPK * ppb
    tok = pl.BlockSpec((1, n_heads, 1, head_dim), lambda b, pt, sl: (b, 0, 0, 0))
    grid_spec = pltpu.PrefetchScalarGridSpec(
        num_scalar_prefetch=2,
        grid=(n,),
        in_specs=[pl.BlockSpec(slopes.shape, lambda b, pt, sl: (0, 0, 0)), tok, tok, tok,
                  pl.BlockSpec(memory_space=pl.ANY), pl.BlockSpec(memory_space=pl.ANY)],
        out_specs=tok,
        scratch_shapes=[pltpu.VMEM((2, n_tiles, head_dim, page_rows), F32),
                        pltpu.VMEM((2, n_tiles, head_dim, page_rows), F32),
                        pltpu.SemaphoreType.DMA((2,))],
    )
    return pl.pallas_call(
        functools.partial(_attn_sample_kernel, pages_per_block=ppb, past_len=past_len),
        out_shape=jax.ShapeDtypeStruct(q4.shape, F32),
        grid_spec=grid_spec,
        compiler_params=_params(("arbitrary",)),
        name="attn_sample",
    )(page_table, sel, slopes, q4, k4, v4, cache_kt, cache_vt)


def kernel(x_prompt, x_sample, cache_k, cache_v, state_pool, page_table, p_prompt, p_sample, w_in, w_branch,
           w_out, ln1_g, ln1_b, w_up, w_down, ln2_g, ln2_b, w_pe, w_pg, w_pool, pool_scale):
    depth = w_in.shape[0]
    batch, seq, d_model = x_prompt.shape
    n_dec, dec_seq, _ = x_sample.shape
    assert dec_seq == 1, "the sample path handles one new token per sequence"
    d_attn = d_model // 2
    d_pool = d_model // 2
    n_heads = d_attn // HEAD_DIM
    n_phys, page_rows = cache_k.shape[1:3]
    past_len = page_table.shape[1] * page_rows
    alpha = (2.0 * depth) ** 0.25
    nblk = seq // MOBA_BLOCK
    nrow = -(-nblk // 16) * 16
    assert past_len // MOBA_BLOCK >= MOBA_TOPK and page_rows <= LANES
    slopes = jnp.asarray([[[_alibi_slope(h, n_heads)] * MOBA_BLOCK] for h in range(n_heads)], F32)

    y_p = x_prompt.reshape(batch * seq, d_model)
    y_s = x_sample.reshape(n_dec, d_model)
    outs = [[] for _ in range(6)]
    for i in range(depth):
        w_in_b = w_in[i].astype(BF16)
        row = lambda a: a[i].reshape(1, -1)
        weights = (w_pool[i].astype(BF16), row(pool_scale), w_branch[i].astype(BF16), w_out[i].astype(BF16),
                   row(ln1_g), row(ln1_b), w_up[i].astype(BF16), w_down[i].astype(BF16),
                   row(ln2_g), row(ln2_b), w_pe[i].astype(BF16), w_pg[i].astype(BF16))

        qs, ks, vs, us, sgs, qg = _proj_sample(y_s, w_in_b, w_in[i][:, :d_attn])
        ckt = cache_k[i].transpose(0, 2, 3, 1)
        cvt = cache_v[i].transpose(0, 2, 3, 1)

        kt, vtf, u, sg, qt, ka, vt, km = _proj_prompt(y_p, w_in_b, batch, seq)
        km = km.reshape(batch, nblk, n_heads, HEAD_DIM).transpose(0, 2, 1, 3)
        km = jnp.pad(km, ((0, 0), (0, 0), (0, nrow - nblk), (0, HEAD_DIM))).astype(BF16)
        yt = _attn_prompt(slopes * LOG2E, qt, ka, vt, km)
        y_p, sel = _tail_prompt(y_p, yt, u, sg, p_prompt[i].reshape(batch * seq, -1), weights, seq, alpha,
                                page_table, qg.reshape(n_dec, d_attn, 1), ckt)
        outs[0].append(kt.transpose(0, 3, 1, 2))
        outs[1].append(vtf.transpose(0, 3, 1, 2))
        outs[2].append(u.reshape(batch, seq, d_pool)[:, seq - POOL_BUF:])

        q4, k4, v4 = (a.reshape(n_dec, n_heads, 1, HEAD_DIM) for a in (qs, ks, vs))
        sel = sel[:, :, :MOBA_TOPK].reshape(-1)
        ya_s = _attn_sample(page_table, sel, slopes, q4, k4, v4, ckt, cvt, past_len).reshape(n_dec, d_attn)
        state = state_pool[i]
        y_s = _tail_sample(y_s, ya_s, us, state.transpose(1, 0, 2), sgs,
                           p_sample[i].reshape(n_dec, -1), weights, past_len, alpha)
        outs[3].append(ks.reshape(n_dec, 1, n_heads, HEAD_DIM))
        outs[4].append(vs.reshape(n_dec, 1, n_heads, HEAD_DIM))
        outs[5].append(jnp.concatenate([state[:, 1:], us[:, None, :]], axis=1))

    k_p, v_p, pool_p, k_s, v_s, pool_s = (jnp.stack(o) for o in outs)
    return (y_p.reshape(batch, seq, d_model), y_s.reshape(n_dec, 1, d_model), k_p, v_p, pool_p, k_s, v_s, pool_s)
```

```python
import functools

import jax
import jax.numpy as jnp
import numpy as np
from jax import lax
from jax.experimental import pallas as pl
from jax.experimental.pallas import tpu as pltpu

HEAD_DIM = 64
MOBA_BLOCK = 256
MOBA_TOPK = 3
POOL_WINDOWS = (2, 4, 8, 16)
POOL_BUF = max(POOL_WINDOWS) - 1
POOL_HALO = 16
LN_EPS = 1e-5
NEG_INF = -1e30
LANES = 128
VMEM_LIMIT_BYTES = 56 * 1024 * 1024

F32 = jnp.float32
BF16 = jnp.bfloat16


def _dot(a, b):
    return jnp.dot(a, b, preferred_element_type=F32)


def _dot_nt(a, b):
    return lax.dot_general(a, b, (((1,), (1,)), ((), ())), preferred_element_type=F32)


def _resident(block_shape, index_map):
    return pl.BlockSpec(block_shape, index_map, pipeline_mode=pl.Buffered(1))


def _params(semantics, flags=None):
    return pltpu.CompilerParams(dimension_semantics=semantics, vmem_limit_bytes=VMEM_LIMIT_BYTES, flags=flags)


def _sigmoid(x):
    return 1.0 / (1.0 + jnp.exp(-x))


def _layer_norm(x, g, b):
    mu = jnp.mean(x, axis=-1, keepdims=True)
    xc = x - mu
    var = jnp.mean(xc * xc, axis=-1, keepdims=True)
    return xc * lax.rsqrt(var + LN_EPS) * g + b


LOG2E = 1.4426950408889634
V_PAD = 16


def _alibi_slope(h, n_heads):
    return 2.0 ** (-8.0 * (h + 1) / n_heads)


def _split_bf16(c):
    hi = float(np.asarray(c, dtype=BF16))
    lo = float(np.asarray(c - hi, dtype=BF16))
    return hi, lo


def _proj_prompt_kernel(x_ref, w_ref, k_ref, v_ref, u_ref, sg_ref, qt_ref, ka_ref, vt_ref, km_ref,
                        *, d_attn, d_pool, n_heads):
    tm = x_ref.shape[0]
    xb = x_ref[...].astype(BF16)
    o_k, o_v, o_u, o_g = d_attn, 2 * d_attn, 3 * d_attn, 3 * d_attn + d_pool

    q = _dot(xb, w_ref[:, 0:o_k]) * (LOG2E * HEAD_DIM ** -0.5)
    qt = q.T
    row = lax.broadcasted_iota(jnp.int32, (HEAD_DIM, tm), 0)
    for h in range(n_heads):
        c_hi, c_lo = _split_bf16(LOG2E * _alibi_slope(h, n_heads))
        qt_ref[0, h, 0:HEAD_DIM, :] = qt[h * HEAD_DIM:(h + 1) * HEAD_DIM, :].astype(BF16)
        extra = jnp.where(row == 0, 16.0 * c_hi, jnp.where(row == 1, c_hi,
                          jnp.where(row == 2, 16.0 * c_lo, jnp.where(row == 3, c_lo, 0.0))))
        qt_ref[0, h, HEAD_DIM:2 * HEAD_DIM, :] = extra.astype(BF16)

    k = _dot(xb, w_ref[:, o_k:o_v])
    kt = k.T
    for h in range(n_heads):
        k_ref[0, h] = kt[h * HEAD_DIM:(h + 1) * HEAD_DIM, :]
    for c in range(tm // MOBA_BLOCK):
        blk = k[c * MOBA_BLOCK:(c + 1) * MOBA_BLOCK, :]
        km_ref[c] = jnp.sum(blk, axis=0, keepdims=True) * (1.0 / MOBA_BLOCK)
    prow = lax.broadcasted_iota(jnp.int32, (tm, LANES), 0) & (MOBA_BLOCK - 1)
    lane = lax.broadcasted_iota(jnp.int32, (tm, LANES), 1)
    sixteens = jnp.logical_or(lane == HEAD_DIM, lane == HEAD_DIM + 2)
    units = jnp.logical_or(lane == HEAD_DIM + 1, lane == HEAD_DIM + 3)
    posf = jnp.where(sixteens, (prow >> 4).astype(F32), jnp.where(units, (prow & 15).astype(F32), 0.0))
    low = lane < HEAD_DIM
    for m in range(n_heads // 2):
        col = k[:, m * LANES:(m + 1) * LANES]
        ka_ref[0, 2 * m] = jnp.where(low, col, posf).astype(BF16)
        ka_ref[0, 2 * m + 1] = jnp.where(low, pltpu.roll(col, HEAD_DIM, 1), posf).astype(BF16)

    vt = _dot(xb, w_ref[:, o_v:o_u]).T
    ones_row = jnp.where(lax.broadcasted_iota(jnp.int32, (V_PAD, MOBA_BLOCK), 0) == 0, 1.0, 0.0).astype(BF16)
    for h in range(n_heads):
        v_ref[0, h] = vt[h * HEAD_DIM:(h + 1) * HEAD_DIM, :]
        for c in range(tm // MOBA_BLOCK):
            vt_ref[0, h, c, 0:HEAD_DIM, :] = vt[h * HEAD_DIM:(h + 1) * HEAD_DIM,
                                                c * MOBA_BLOCK:(c + 1) * MOBA_BLOCK].astype(BF16)
            vt_ref[0, h, c, HEAD_DIM:HEAD_DIM + V_PAD, :] = ones_row

    u_ref[...] = _dot(xb, w_ref[:, o_u:o_g])
    sg_ref[...] = _sigmoid(_dot(xb, w_ref[:, o_g:])).astype(BF16)


def _proj_prompt(x2d, w_in, batch, seq, tm=512):
    t, d_model = x2d.shape
    d_attn = d_model // 2
    d_pool = d_model // 2
    n_heads = d_attn // HEAD_DIM
    d_in = w_in.shape[1]
    n_gate = d_in - 3 * d_attn - d_pool
    nt = seq // tm
    nblk = seq // MOBA_BLOCK
    cb = tm // MOBA_BLOCK
    out_shape = (
        jax.ShapeDtypeStruct((batch, n_heads, HEAD_DIM, seq), F32),
        jax.ShapeDtypeStruct((batch, n_heads, HEAD_DIM, seq), F32),
        jax.ShapeDtypeStruct((t, d_pool), F32),
        jax.ShapeDtypeStruct((t, n_gate), BF16),
        jax.ShapeDtypeStruct((batch, n_heads, 2 * HEAD_DIM, seq), BF16),
        jax.ShapeDtypeStruct((batch, n_heads, seq, 2 * HEAD_DIM), BF16),
        jax.ShapeDtypeStruct((batch, n_heads, nblk, HEAD_DIM + V_PAD, MOBA_BLOCK), BF16),
        jax.ShapeDtypeStruct((t // MOBA_BLOCK, 1, d_attn), F32),
    )
    out_specs = (
        pl.BlockSpec((1, n_heads, HEAD_DIM, tm), lambda i: (i // nt, 0, 0, i % nt)),
        pl.BlockSpec((1, n_heads, HEAD_DIM, tm), lambda i: (i // nt, 0, 0, i % nt)),
        pl.BlockSpec((tm, d_pool), lambda i: (i, 0)),
        pl.BlockSpec((tm, n_gate), lambda i: (i, 0)),
        pl.BlockSpec((1, n_heads, 2 * HEAD_DIM, tm), lambda i: (i // nt, 0, 0, i % nt)),
        pl.BlockSpec((1, n_heads, tm, 2 * HEAD_DIM), lambda i: (i // nt, 0, i % nt, 0)),
        pl.BlockSpec((1, n_heads, cb, HEAD_DIM + V_PAD, MOBA_BLOCK), lambda i: (i // nt, 0, i % nt, 0, 0)),
        pl.BlockSpec((cb, 1, d_attn), lambda i: (i, 0, 0)),
    )
    return pl.pallas_call(
        functools.partial(_proj_prompt_kernel, d_attn=d_attn, d_pool=d_pool, n_heads=n_heads),
        out_shape=out_shape,
        grid=(t // tm,),
        in_specs=[pl.BlockSpec((tm, d_model), lambda i: (i, 0)),
                  _resident((d_model, d_in), lambda i: (0, 0))],
        out_specs=out_specs,
        compiler_params=_params(("arbitrary",)),
        name="proj_prompt",
    )(x2d, w_in)


def _attn_prompt_kernel(slope_ref, qt_ref, ka_ref, vt_ref, km_ref, o_ref, mask_ref, m_ref, acc_ref, s_ref):
    qi = pl.program_id(1)
    n_heads, nrow, tq = mask_ref.shape
    own = 2
    row = lax.broadcasted_iota(jnp.int32, (nrow, tq), 0).astype(F32)
    qif = jnp.full((1, tq), qi, jnp.int32).astype(F32)
    ik = lax.broadcasted_iota(jnp.int32, (MOBA_BLOCK, tq), 0)
    iq = lax.broadcasted_iota(jnp.int32, (MOBA_BLOCK, tq), 1)

    def scores(j, slot, heads=range(n_heads)):
        off = pl.multiple_of(j * MOBA_BLOCK, MOBA_BLOCK)
        for h in heads:
            s_ref[slot, h] = _dot(ka_ref[0, h, pl.ds(off, MOBA_BLOCK), :], qt_ref[0, h])

    def absorb(j, slot, heads=range(n_heads)):
        for h in heads:
            ps, alphas = [], []
            for c in range(tq // LANES):
                cols = slice(c * LANES, (c + 1) * LANES)
                s = s_ref[slot, h, :, cols]
                bias = mask_ref[h, pl.ds(j, 1), :][:, cols]
                m = m_ref[h, :, cols]
                m_new = jnp.maximum(m, jnp.max(s, axis=0, keepdims=True) + bias)
                alphas.append(jnp.exp2(m - m_new))
                ps.append(jnp.exp2(s - (m_new - bias)).astype(BF16))
                m_ref[h, :, cols] = m_new
            p = jnp.concatenate(ps, axis=1)
            acc_ref[h] = jnp.concatenate(alphas, axis=1) * acc_ref[h] + _dot(vt_ref[0, h, j], p)

    gates = [_dot(km_ref[0, h], qt_ref[0, h]) for h in range(n_heads)]
    for h in range(n_heads):
        scores(qi, own, [h])
        g = jnp.where(row < qif, gates[h], NEG_INF)
        sel = jnp.zeros((nrow, tq), jnp.bool_)
        for r in range(MOBA_TOPK):
            mx = jnp.max(g, axis=0, keepdims=True)
            idx = jnp.min(jnp.where(g == mx, row, float(nrow)), axis=0, keepdims=True)
            hit = row == idx
            sel = jnp.logical_or(sel, jnp.logical_and(hit, qif > float(r)))
            g = jnp.where(hit, -jnp.inf, g)
        mask_ref[h] = jnp.where(sel, slope_ref[h] * float(MOBA_BLOCK) * (row - qif), NEG_INF)

    for h in range(n_heads):
        scores(0, 0, [h])
        s = jnp.where(ik <= iq, s_ref[own, h], NEG_INF)
        m0 = jnp.max(s, axis=0, keepdims=True)
        p = jnp.exp2(s - m0)
        m_ref[h] = m0
        acc_ref[h] = _dot(vt_ref[0, h, qi], p.astype(BF16))

    for h in range(n_heads):
        scores(1, 1, [h])
        absorb(0, 0, [h])

    def body(jj, carry):
        j = 2 * jj
        for h in range(n_heads):
            scores(j, 0, [h])
            absorb(j - 1, 1, [h])
        for h in range(n_heads):
            scores(j + 1, 1, [h])
            absorb(j, 0, [h])
        return carry

    pairs = jnp.maximum((qi + 1) >> 1, 1)
    lax.fori_loop(1, pairs, body, 0)
    last = 2 * pairs - 1
    for h in range(n_heads):
        absorb(last, 1, [h])
        o_ref[0, h] = acc_ref[h, 0:HEAD_DIM, :] * (1.0 / acc_ref[h, HEAD_DIM:HEAD_DIM + 1, :])


def _attn_prompt(slopes, qt, ka, vt, km):
    batch, n_heads, _, seq = qt.shape
    nblk = seq // MOBA_BLOCK
    nrow = km.shape[2]
    per_batch = lambda nd: (lambda b, i: (b,) + (0,) * (nd - 1))
    return pl.pallas_call(
        _attn_prompt_kernel,
        out_shape=jax.ShapeDtypeStruct((batch, n_heads, HEAD_DIM, seq), F32),
        grid=(batch, nblk),
        in_specs=[
            pl.BlockSpec(slopes.shape, lambda b, i: (0, 0, 0)),
            pl.BlockSpec((1, n_heads, 2 * HEAD_DIM, MOBA_BLOCK), lambda b, i: (b, 0, 0, i)),
            _resident((1,) + ka.shape[1:], per_batch(4)),
            _resident((1,) + vt.shape[1:], per_batch(5)),
            pl.BlockSpec((1,) + km.shape[1:], per_batch(4)),
        ],
        out_specs=pl.BlockSpec((1, n_heads, HEAD_DIM, MOBA_BLOCK), lambda b, i: (b, 0, 0, i)),
        scratch_shapes=[pltpu.VMEM((n_heads, nrow, MOBA_BLOCK), F32),
                        pltpu.VMEM((n_heads, 1, MOBA_BLOCK), F32),
                        pltpu.VMEM((n_heads, HEAD_DIM + V_PAD, MOBA_BLOCK), F32),
                        pltpu.VMEM((3, n_heads, MOBA_BLOCK, MOBA_BLOCK), F32)],
        compiler_params=_params(("arbitrary", "arbitrary")),
        name="attn_prompt",
    )(slopes, qt, ka, vt, km)


def _tail(x, ya_b, pool_d, sg_ref, p_ref, wp_ref, ps_ref, wb_ref, wo_ref, g1_ref, b1_ref, wu_ref, wd_ref,
          g2_ref, b2_ref, wpe_ref, wpg_ref, o_ref, yb_ref, *, alpha, ff_chunk, side_work=(), after_mlp=None):
    d_model = x.shape[1]
    pool_ch = wp_ref.shape[1]
    d = pool_d()
    for g in range(len(POOL_WINDOWS)):
        sl = slice(g * pool_ch, (g + 1) * pool_ch)
        yb = _dot(d[g].astype(BF16), wp_ref[g]) * ps_ref[:, sl]
        yb_ref[:, sl] = yb.astype(BF16)
    br0 = _dot(ya_b, wb_ref[0])
    br1 = _dot(yb_ref[...], wb_ref[1])
    merged = sg_ref[:, 0:d_model].astype(F32) * br0 + sg_ref[:, d_model:].astype(F32) * br1
    mix = _dot(merged.astype(BF16), wo_ref[...])
    pe = _dot(p_ref[...].astype(BF16), wpe_ref[...])
    h1 = _layer_norm(alpha * x + mix, g1_ref[...], b1_ref[...])
    h1b = h1.astype(BF16)
    ple = _sigmoid(_dot(h1b, wpg_ref[...])) * pe
    acc = alpha * h1 + ple
    d_ff = wu_ref.shape[1]
    assert len(side_work) <= d_ff // ff_chunk
    for c in range(d_ff // ff_chunk):
        if c < len(side_work):
            side_work[c]()
        a = jnp.maximum(_dot(h1b, wu_ref[:, c * ff_chunk:(c + 1) * ff_chunk]), 0.0)
        acc = acc + _dot((a * a).astype(BF16), wd_ref[c * ff_chunk:(c + 1) * ff_chunk, :])
    if after_mlp is not None:
        after_mlp()
    o_ref[...] = _layer_norm(acc, g2_ref[...], b2_ref[...])


def _rank_sample_jobs(pages, n_pages, q_ref, g_ref, idx_ref, chunk, pages_per_block, nblk, n_jobs):
    blocks = n_pages // pages_per_block
    assert blocks % n_jobs == 0
    per_job = blocks // n_jobs

    def job(k):
        first = pages(k * per_job * pages_per_block)
        n_heads, head_dim, page_rows = first.shape
        lane = lax.broadcasted_iota(jnp.int32, (n_heads, LANES), 1)
        qcol = jnp.broadcast_to(q_ref[0], (n_heads * head_dim, page_rows)).reshape(n_heads, head_dim, page_rows)
        g = g_ref[...]
        if k == 0:
            g = jnp.where(chunk == 0, -jnp.inf, g)
        for r in range(k * per_job, (k + 1) * per_job):
            ksum = first if r == k * per_job else pages(r * pages_per_block)
            for s in range(1, pages_per_block):
                ksum = ksum + pages(r * pages_per_block + s)
            per_key = jnp.sum(ksum * qcol, axis=1)
            gate = jnp.sum(per_key, axis=1, keepdims=True) * (1.0 / (page_rows * pages_per_block))
            g = jnp.where(lane == chunk * blocks + r, gate, g)
        g_ref[...] = g
        if k < n_jobs - 1:
            return
        col = lane.astype(F32)
        out = jnp.zeros((n_heads, LANES), F32)
        for r in range(MOBA_TOPK):
            mx = jnp.max(g, axis=1, keepdims=True)
            idx = jnp.min(jnp.where(g == mx, col, float(LANES)), axis=1, keepdims=True)
            out = jnp.where(lane == r, idx, out)
            g = jnp.where(col == idx, -jnp.inf, g)
        idx_ref[0] = jnp.minimum(out.astype(jnp.int32), nblk - 1)

    return [functools.partial(job, k) for k in range(n_jobs)]


def _tail_prompt_kernel(pt_ref, x_ref, yt_ref, u_ref, halo_ref, sg_ref, p_ref, q_ref, ck_ref, *rest,
                        tiles_per_seq, chunks_per_sample, pages_per_block, nblk_sample, alpha, ff_chunk):
    *w_refs, o_ref, idx_ref, yb_ref, ue_ref, g_ref, page_buf, sem = rest
    i = pl.program_id(0)
    slot = i & 1
    n_pages = page_buf.shape[1]

    def page_copy(step, slot_, r):
        return pltpu.make_async_copy(ck_ref.at[pt_ref[step, r]], page_buf.at[slot_, r], sem.at[slot_])

    @pl.when(i == 0)
    def _():
        for r in range(n_pages):
            page_copy(0, 0, r).start()

    def fetch_next():
        @pl.when(i + 1 < pl.num_programs(0))
        def _():
            for r in range(n_pages):
                page_copy(i + 1, 1 - slot, r).start()

    blocks = n_pages // pages_per_block
    n_mlp_chunks = w_refs[6].shape[1] // ff_chunk
    n_jobs = max(j for j in range(1, n_mlp_chunks + 1) if blocks % j == 0)
    jobs = _rank_sample_jobs(lambda r: page_buf[slot, r], n_pages, q_ref, g_ref, idx_ref, i % chunks_per_sample,
                             pages_per_block, nblk_sample, n_jobs)

    def first_job():
        for r in range(n_pages):
            page_copy(i, slot, r).wait()
        jobs[0]()

    side_work = [first_job] + jobs[1:]

    tm = x_ref.shape[0]
    ti = pl.program_id(0) % tiles_per_seq

    def pool_d():
        u = u_ref[...]
        ue_ref[POOL_HALO:POOL_HALO + tm, :] = u
        ue_ref[0:POOL_HALO, :] = jnp.where(ti == 0, 0.0, halo_ref[...])
        pool_ch = u.shape[1] // len(POOL_WINDOWS)
        pos = ti * tm + lax.broadcasted_iota(jnp.int32, (tm, pool_ch), 0)
        d = []
        for g, w in enumerate(POOL_WINDOWS):
            sl = slice(g * pool_ch, (g + 1) * pool_ch)
            pooled = u[:, sl]
            for s in range(1, w):
                pooled = pooled + ue_ref[POOL_HALO - s:POOL_HALO - s + tm, sl]
            cnt = jnp.minimum(w, pos + 1).astype(F32)
            d.append(pooled / cnt - u[:, sl])
        return d

    ya_b = yt_ref[0].T.astype(BF16)
    _tail(x_ref[...], ya_b, pool_d, sg_ref, p_ref, *w_refs, o_ref, yb_ref, alpha=alpha, ff_chunk=ff_chunk,
          side_work=side_work, after_mlp=fetch_next)


def _tail_sample_kernel(x_ref, ya_ref, u_ref, st_ref, sg_ref, p_ref, *rest, cnts, alpha, ff_chunk):
    *w_refs, o_ref, yb_ref = rest
    def pool_d():
        u = u_ref[...]
        pool_ch = u.shape[1] // len(POOL_WINDOWS)
        d = []
        for g, w in enumerate(POOL_WINDOWS):
            pooled = u[:, g * pool_ch:(g + 1) * pool_ch]
            for s in range(1, w):
                pooled = pooled + st_ref[POOL_BUF - s][:, g * pool_ch:(g + 1) * pool_ch]
            d.append(pooled / cnts[g] - u[:, g * pool_ch:(g + 1) * pool_ch])
        return d

    _tail(x_ref[...], ya_ref[...].astype(BF16), pool_d, sg_ref, p_ref, *w_refs, o_ref, yb_ref,
          alpha=alpha, ff_chunk=ff_chunk)


def _weight_specs(weights, index_map_factory):
    return [_resident(w.shape, index_map_factory(w.ndim)) for w in weights]


def _tail_prompt(x2d, yt, u, sg, p2d, weights, seq, alpha, page_table, qcol, cache_kt, tm=256, ff_chunk=1024):
    t, d_model = x2d.shape
    d_pool = u.shape[1]
    batch = t // seq
    nt = seq // tm
    hb = tm // POOL_HALO
    steps = t // tm
    n_dec, n_pages = page_table.shape
    n_heads, head_dim, page_rows = cache_kt.shape[1:]
    ppb = MOBA_BLOCK // page_rows
    nblk = n_pages // ppb
    assert steps % n_dec == 0 and MOBA_TOPK <= nblk <= LANES
    cps = steps // n_dec
    assert n_pages % (cps * ppb) == 0
    pps = n_pages // cps
    yt3 = yt.reshape(batch, yt.shape[1] * yt.shape[2], seq)
    pages_by_step = page_table.reshape(steps, pps)
    in_specs = [
        pl.BlockSpec((tm, d_model), lambda i, pt: (i, 0)),
        pl.BlockSpec((1, yt3.shape[1], tm), lambda i, pt: (i // nt, 0, i % nt)),
        pl.BlockSpec((tm, d_pool), lambda i, pt: (i, 0)),
        pl.BlockSpec((POOL_HALO, d_pool), lambda i, pt: (jnp.maximum(i * hb - 1, 0), 0)),
        pl.BlockSpec((tm, sg.shape[1]), lambda i, pt: (i, 0)),
        pl.BlockSpec((tm, p2d.shape[1]), lambda i, pt: (i, 0)),
        pl.BlockSpec((1, n_heads * head_dim, 1), lambda i, pt: (i // cps, 0, 0)),
        pl.BlockSpec(memory_space=pl.ANY),
    ] + _weight_specs(weights, lambda nd: (lambda i, pt: (0,) * nd))
    grid_spec = pltpu.PrefetchScalarGridSpec(
        num_scalar_prefetch=1,
        grid=(steps,),
        in_specs=in_specs,
        out_specs=(pl.BlockSpec((tm, d_model), lambda i, pt: (i, 0)),
                   pl.BlockSpec((1, n_heads, LANES), lambda i, pt: (i // cps, 0, 0))),
        scratch_shapes=[pltpu.VMEM((tm, d_pool), BF16), pltpu.VMEM((POOL_HALO + tm, d_pool), F32),
                        pltpu.VMEM((n_heads, LANES), F32),
                        pltpu.VMEM((2, pps, n_heads, head_dim, page_rows), F32),
                        pltpu.SemaphoreType.DMA((2,))],
    )
    return pl.pallas_call(
        functools.partial(_tail_prompt_kernel, tiles_per_seq=nt, chunks_per_sample=cps,
                          pages_per_block=ppb, nblk_sample=nblk, alpha=alpha, ff_chunk=ff_chunk),
        out_shape=(jax.ShapeDtypeStruct((t, d_model), F32),
                   jax.ShapeDtypeStruct((n_dec, n_heads, LANES), jnp.int32)),
        grid_spec=grid_spec,
        compiler_params=_params(("arbitrary",)),
        name="tail_prompt",
    )(pages_by_step, x2d, yt3, u, u, sg, p2d, qcol, cache_kt, *weights)


def _tail_sample(x2d, ya, u, state_t, sg, p2d, weights, past_len, alpha, ff_chunk=1024):
    n, d_model = x2d.shape
    d_pool = u.shape[1]
    cnts = tuple(float(min(w, past_len + 1)) for w in POOL_WINDOWS)
    operands = (x2d, ya, u, state_t, sg, p2d)
    whole = lambda nd: (lambda i: (0,) * nd)
    in_specs = [pl.BlockSpec(a.shape, whole(a.ndim)) for a in operands] + _weight_specs(weights, whole)
    return pl.pallas_call(
        functools.partial(_tail_sample_kernel, cnts=cnts, alpha=alpha, ff_chunk=ff_chunk),
        out_shape=jax.ShapeDtypeStruct((n, d_model), F32),
        grid=(1,),
        in_specs=in_specs,
        out_specs=pl.BlockSpec((n, d_model), lambda i: (0, 0)),
        scratch_shapes=[pltpu.VMEM((n, d_pool), BF16)],
        compiler_params=_params(("arbitrary",)),
        name="tail_sample",
    )(*operands, *weights)


def _proj_sample_kernel(x_ref, w_ref, wq_ref, q_ref, k_ref, v_ref, u_ref, sg_ref, qg_ref, *, d_attn, d_pool):
    x = x_ref[...]
    z = _dot(x.astype(BF16), w_ref[...])
    q_ref[...] = z[:, 0:d_attn] * (HEAD_DIM ** -0.5)
    k_ref[...] = z[:, d_attn:2 * d_attn]
    v_ref[...] = z[:, 2 * d_attn:3 * d_attn]
    u_ref[...] = z[:, 3 * d_attn:3 * d_attn + d_pool]
    sg_ref[...] = _sigmoid(z[:, 3 * d_attn + d_pool:]).astype(BF16)
    qg_ref[...] = jnp.dot(x, wq_ref[...], precision=lax.Precision.HIGHEST, preferred_element_type=F32)


def _proj_sample(x2d, w_in, wq_f32):
    n, d_model = x2d.shape
    d_attn = d_model // 2
    d_pool = d_model // 2
    n_gate = w_in.shape[1] - 3 * d_attn - d_pool
    shapes = [(n, d_attn)] * 3 + [(n, d_pool)]
    whole = lambda s: pl.BlockSpec(s, lambda i: (0, 0))
    return pl.pallas_call(
        functools.partial(_proj_sample_kernel, d_attn=d_attn, d_pool=d_pool),
        out_shape=tuple(jax.ShapeDtypeStruct(s, F32) for s in shapes)
        + (jax.ShapeDtypeStruct((n, n_gate), BF16), jax.ShapeDtypeStruct((n, d_attn), F32)),
        grid=(1,),
        in_specs=[whole(x2d.shape), whole(w_in.shape), whole(wq_f32.shape)],
        out_specs=tuple(whole(s) for s in shapes) + (whole((n, n_gate)), whole((n, d_attn))),
        compiler_params=_params(("arbitrary",)),
        name="proj_sample",
    )(x2d, w_in, wq_f32)


def _attn_sample_kernel(pt_ref, sel_ref, slope_ref, q_ref, kn_ref, vn_ref, ck_ref, cv_ref, o_ref,
                        kbuf, vbuf, sem, *, pages_per_block, past_len):
    b = pl.program_id(0)
    n_seq = pl.num_programs(0)
    n_heads = q_ref.shape[1]
    head_dim, page_rows = kbuf.shape[2:]
    tiles_per_head = MOBA_TOPK * pages_per_block
    slot = b & 1

    def tile_copies(seq, slot_, h, t):
        r, s = divmod(t, pages_per_block)
        blk = sel_ref[(seq * n_heads + h) * MOBA_TOPK + r]
        page = pt_ref[seq, blk * pages_per_block + s]
        i = h * tiles_per_head + t
        return (pltpu.make_async_copy(ck_ref.at[page, h], kbuf.at[slot_, i], sem.at[slot_]),
                pltpu.make_async_copy(cv_ref.at[page, h], vbuf.at[slot_, i], sem.at[slot_]))

    def start_gather(seq, slot_):
        for h in range(n_heads):
            for t in range(tiles_per_head):
                for copy in tile_copies(seq, slot_, h, t):
                    copy.start()

    @pl.when(b == 0)
    def _():
        start_gather(0, 0)

    @pl.when(b + 1 < n_seq)
    def _():
        start_gather(b + 1, 1 - slot)

    for h in range(n_heads):
        for t in range(tiles_per_head):
            for copy in tile_copies(b, slot, h, t):
                copy.wait()

    kpos = lax.broadcasted_iota(jnp.int32, (1, page_rows), 1)
    for h in range(n_heads):
        q = q_ref[0, h].astype(BF16)
        q8 = jnp.broadcast_to(q, (8, head_dim))
        slope = slope_ref[h][:, 0:page_rows]
        s_self = jnp.sum(q.astype(F32) * kn_ref[0, h].astype(BF16).astype(F32), axis=1, keepdims=True)
        scores = []
        m = s_self
        for t in range(tiles_per_head):
            r, s = divmod(t, pages_per_block)
            blk = sel_ref[(b * n_heads + h) * MOBA_TOPK + r]
            kt = kbuf[slot, h * tiles_per_head + t].astype(BF16)
            dist = past_len - (blk * MOBA_BLOCK + s * page_rows + kpos)
            sc = _dot(q8, kt)[0:1, :] - slope * dist.astype(F32)
            scores.append(sc)
            m = jnp.maximum(m, jnp.max(sc, axis=1, keepdims=True))
        p_self = jnp.exp(s_self - m)
        l = p_self
        acc = p_self * vn_ref[0, h]
        for t, sc in enumerate(scores):
            p = jnp.exp(sc - m)
            l = l + jnp.sum(p, axis=1, keepdims=True)
            p8 = jnp.broadcast_to(p.astype(BF16), (8, page_rows))
            acc = acc + _dot_nt(p8, vbuf[slot, h * tiles_per_head + t].astype(BF16))[0:1, :]
        o_ref[0, h] = acc / l


def _attn_sample(page_table, sel, slopes, q4, k4, v4, cache_kt, cache_vt, past_len):
    n, n_heads, _, head_dim = q4.shape
    page_rows = cache_kt.shape[3]
    ppb = MOBA_BLOCK // page_rows
    n_tiles = n_heads * MOBA_TOPK * ppb
    tok = pl.BlockSpec((1, n_heads, 1, head_dim), lambda b, pt, sl: (b, 0, 0, 0))
    grid_spec = pltpu.PrefetchScalarGridSpec(
        num_scalar_prefetch=2,
        grid=(n,),
        in_specs=[pl.BlockSpec(slopes.shape, lambda b, pt, sl: (0, 0, 0)), tok, tok, tok,
                  pl.BlockSpec(memory_space=pl.ANY), pl.BlockSpec(memory_space=pl.ANY)],
        out_specs=tok,
        scratch_shapes=[pltpu.VMEM((2, n_tiles, head_dim, page_rows), F32),
                        pltpu.VMEM((2, n_tiles, head_dim, page_rows), F32),
                        pltpu.SemaphoreType.DMA((2,))],
    )
    return pl.pallas_call(
        functools.partial(_attn_sample_kernel, pages_per_block=ppb, past_len=past_len),
        out_shape=jax.ShapeDtypeStruct(q4.shape, F32),
        grid_spec=grid_spec,
        compiler_params=_params(("arbitrary",)),
        name="attn_sample",
    )(page_table, sel, slopes, q4, k4, v4, cache_kt, cache_vt)


def kernel(x_prompt, x_sample, cache_k, cache_v, state_pool, page_table, p_prompt, p_sample, w_in, w_branch,
           w_out, ln1_g, ln1_b, w_up, w_down, ln2_g, ln2_b, w_pe, w_pg, w_pool, pool_scale):
    depth = w_in.shape[0]
    batch, seq, d_model = x_prompt.shape
    n_dec, dec_seq, _ = x_sample.shape
    assert dec_seq == 1, "the sample path handles one new token per sequence"
    d_attn = d_model // 2
    d_pool = d_model // 2
    n_heads = d_attn // HEAD_DIM
    n_phys, page_rows = cache_k.shape[1:3]
    past_len = page_table.shape[1] * page_rows
    alpha = (2.0 * depth) ** 0.25
    nblk = seq // MOBA_BLOCK
    nrow = -(-nblk // 16) * 16
    assert past_len // MOBA_BLOCK >= MOBA_TOPK and page_rows <= LANES
    slopes = jnp.asarray([[[_alibi_slope(h, n_heads)] * MOBA_BLOCK] for h in range(n_heads)], F32)

    y_p = x_prompt.reshape(batch * seq, d_model)
    y_s = x_sample.reshape(n_dec, d_model)
    outs = [[] for _ in range(6)]
    for i in range(depth):
        w_in_b = w_in[i].astype(BF16)
        row = lambda a: a[i].reshape(1, -1)
        weights = (w_pool[i].astype(BF16), row(pool_scale), w_branch[i].astype(BF16), w_out[i].astype(BF16),
                   row(ln1_g), row(ln1_b), w_up[i].astype(BF16), w_down[i].astype(BF16),
                   row(ln2_g), row(ln2_b), w_pe[i].astype(BF16), w_pg[i].astype(BF16))

        qs, ks, vs, us, sgs, qg = _proj_sample(y_s, w_in_b, w_in[i][:, :d_attn])
        ckt = cache_k[i].transpose(0, 2, 3, 1)
        cvt = cache_v[i].transpose(0, 2, 3, 1)

        kt, vtf, u, sg, qt, ka, vt, km = _proj_prompt(y_p, w_in_b, batch, seq)
        km = km.reshape(batch, nblk, n_heads, HEAD_DIM).transpose(0, 2, 1, 3)
        km = jnp.pad(km, ((0, 0), (0, 0), (0, nrow - nblk), (0, HEAD_DIM))).astype(BF16)
        yt = _attn_prompt(slopes * LOG2E, qt, ka, vt, km)
        y_p, sel = _tail_prompt(y_p, yt, u, sg, p_prompt[i].reshape(batch * seq, -1), weights, seq, alpha,
                                page_table, qg.reshape(n_dec, d_attn, 1), ckt)
        outs[0].append(kt.transpose(0, 3, 1, 2))
        outs[1].append(vtf.transpose(0, 3, 1, 2))
        outs[2].append(u.reshape(batch, seq, d_pool)[:, seq - POOL_BUF:])

        q4, k4, v4 = (a.reshape(n_dec, n_heads, 1, HEAD_DIM) for a in (qs, ks, vs))
        sel = sel[:, :, :MOBA_TOPK].reshape(-1)
        ya_s = _attn_sample(page_table, sel, slopes, q4, k4, v4, ckt, cvt, past_len).reshape(n_dec, d_attn)
        state = state_pool[i]
        y_s = _tail_sample(y_s, ya_s, us, state.transpose(1, 0, 2), sgs,
                           p_sample[i].reshape(n_dec, -1), weights, past_len, alpha)
        outs[3].append(ks.reshape(n_dec, 1, n_heads, HEAD_DIM))
        outs[4].append(vs.reshape(n_dec, 1, n_heads, HEAD_DIM))
        outs[5].append(jnp.concatenate([state[:, 1:], us[:, None, :]], axis=1))

    k_p, v_p, pool_p, k_s, v_s, pool_s = (jnp.stack(o) for o in outs)
    return (y_p.reshape(batch, seq, d_model), y_s.reshape(n_dec, 1, d_model), k_p, v_p, pool_p, k_s, v_s, pool_s)
```

```python
import functools

import jax
import jax.numpy as jnp
import numpy as np
from jax import lax
from jax.experimental import pallas as pl
from jax.experimental.pallas import tpu as pltpu

HEAD_DIM = 64
MOBA_BLOCK = 256
MOBA_TOPK = 3
POOL_WINDOWS = (2, 4, 8, 16)
POOL_BUF = max(POOL_WINDOWS) - 1
POOL_HALO = 16
LN_EPS = 1e-5
NEG_INF = -1e30
LANES = 128
VMEM_LIMIT_BYTES = 56 * 1024 * 1024

F32 = jnp.float32
BF16 = jnp.bfloat16


def _dot(a, b):
    return jnp.dot(a, b, preferred_element_type=F32)


def _dot_nt(a, b):
    return lax.dot_general(a, b, (((1,), (1,)), ((), ())), preferred_element_type=F32)


def _resident(block_shape, index_map):
    return pl.BlockSpec(block_shape, index_map, pipeline_mode=pl.Buffered(1))


def _params(semantics):
    return pltpu.CompilerParams(dimension_semantics=semantics, vmem_limit_bytes=VMEM_LIMIT_BYTES)


def _sigmoid(x):
    return 1.0 / (1.0 + jnp.exp(-x))


def _layer_norm(x, g, b):
    mu = jnp.mean(x, axis=-1, keepdims=True)
    xc = x - mu
    var = jnp.mean(xc * xc, axis=-1, keepdims=True)
    return xc * lax.rsqrt(var + LN_EPS) * g + b


LOG2E = 1.4426950408889634
V_PAD = 16


def _alibi_slope(h, n_heads):
    return 2.0 ** (-8.0 * (h + 1) / n_heads)


def _split_bf16(c):
    hi = float(np.asarray(c, dtype=BF16))
    lo = float(np.asarray(c - hi, dtype=BF16))
    return hi, lo


def _proj_prompt_kernel(x_ref, w_ref, k_ref, v_ref, u_ref, sg_ref, qt_ref, ka_ref, vt_ref, km_ref,
                        *, d_attn, d_pool, n_heads):
    tm = x_ref.shape[0]
    xb = x_ref[...].astype(BF16)
    o_k, o_v, o_u, o_g = d_attn, 2 * d_attn, 3 * d_attn, 3 * d_attn + d_pool

    q = _dot(xb, w_ref[:, 0:o_k]) * (LOG2E * HEAD_DIM ** -0.5)
    qt = q.T
    row = lax.broadcasted_iota(jnp.int32, (HEAD_DIM, tm), 0)
    for h in range(n_heads):
        c_hi, c_lo = _split_bf16(LOG2E * _alibi_slope(h, n_heads))
        qt_ref[0, h, 0:HEAD_DIM, :] = qt[h * HEAD_DIM:(h + 1) * HEAD_DIM, :].astype(BF16)
        extra = jnp.where(row == 0, 16.0 * c_hi, jnp.where(row == 1, c_hi,
                          jnp.where(row == 2, 16.0 * c_lo, jnp.where(row == 3, c_lo, 0.0))))
        qt_ref[0, h, HEAD_DIM:2 * HEAD_DIM, :] = extra.astype(BF16)

    k = _dot(xb, w_ref[:, o_k:o_v])
    kt = k.T
    for h in range(n_heads):
        k_ref[0, h] = kt[h * HEAD_DIM:(h + 1) * HEAD_DIM, :]
    for c in range(tm // MOBA_BLOCK):
        blk = k[c * MOBA_BLOCK:(c + 1) * MOBA_BLOCK, :]
        km_ref[c] = jnp.sum(blk, axis=0, keepdims=True) * (1.0 / MOBA_BLOCK)
    prow = lax.broadcasted_iota(jnp.int32, (tm, LANES), 0) & (MOBA_BLOCK - 1)
    lane = lax.broadcasted_iota(jnp.int32, (tm, LANES), 1)
    sixteens = jnp.logical_or(lane == HEAD_DIM, lane == HEAD_DIM + 2)
    units = jnp.logical_or(lane == HEAD_DIM + 1, lane == HEAD_DIM + 3)
    posf = jnp.where(sixteens, (prow >> 4).astype(F32), jnp.where(units, (prow & 15).astype(F32), 0.0))
    low = lane < HEAD_DIM
    for m in range(n_heads // 2):
        col = k[:, m * LANES:(m + 1) * LANES]
        ka_ref[0, 2 * m] = jnp.where(low, col, posf).astype(BF16)
        ka_ref[0, 2 * m + 1] = jnp.where(low, pltpu.roll(col, HEAD_DIM, 1), posf).astype(BF16)

    vt = _dot(xb, w_ref[:, o_v:o_u]).T
    ones_row = jnp.where(lax.broadcasted_iota(jnp.int32, (V_PAD, MOBA_BLOCK), 0) == 0, 1.0, 0.0).astype(BF16)
    for h in range(n_heads):
        v_ref[0, h] = vt[h * HEAD_DIM:(h + 1) * HEAD_DIM, :]
        for c in range(tm // MOBA_BLOCK):
            vt_ref[0, h, c, 0:HEAD_DIM, :] = vt[h * HEAD_DIM:(h + 1) * HEAD_DIM,
                                                c * MOBA_BLOCK:(c + 1) * MOBA_BLOCK].astype(BF16)
            vt_ref[0, h, c, HEAD_DIM:HEAD_DIM + V_PAD, :] = ones_row

    u_ref[...] = _dot(xb, w_ref[:, o_u:o_g])
    sg_ref[...] = _sigmoid(_dot(xb, w_ref[:, o_g:])).astype(BF16)


def _proj_prompt(x2d, w_in, batch, seq, tm=512):
    t, d_model = x2d.shape
    d_attn = d_model // 2
    d_pool = d_model // 2
    n_heads = d_attn // HEAD_DIM
    d_in = w_in.shape[1]
    n_gate = d_in - 3 * d_attn - d_pool
    nt = seq // tm
    nblk = seq // MOBA_BLOCK
    cb = tm // MOBA_BLOCK
    out_shape = (
        jax.ShapeDtypeStruct((batch, n_heads, HEAD_DIM, seq), F32),
        jax.ShapeDtypeStruct((batch, n_heads, HEAD_DIM, seq), F32),
        jax.ShapeDtypeStruct((t, d_pool), F32),
        jax.ShapeDtypeStruct((t, n_gate), BF16),
        jax.ShapeDtypeStruct((batch, n_heads, 2 * HEAD_DIM, seq), BF16),
        jax.ShapeDtypeStruct((batch, n_heads, seq, 2 * HEAD_DIM), BF16),
        jax.ShapeDtypeStruct((batch, n_heads, nblk, HEAD_DIM + V_PAD, MOBA_BLOCK), BF16),
        jax.ShapeDtypeStruct((t // MOBA_BLOCK, 1, d_attn), F32),
    )
    out_specs = (
        pl.BlockSpec((1, n_heads, HEAD_DIM, tm), lambda i: (i // nt, 0, 0, i % nt)),
        pl.BlockSpec((1, n_heads, HEAD_DIM, tm), lambda i: (i // nt, 0, 0, i % nt)),
        pl.BlockSpec((tm, d_pool), lambda i: (i, 0)),
        pl.BlockSpec((tm, n_gate), lambda i: (i, 0)),
        pl.BlockSpec((1, n_heads, 2 * HEAD_DIM, tm), lambda i: (i // nt, 0, 0, i % nt)),
        pl.BlockSpec((1, n_heads, tm, 2 * HEAD_DIM), lambda i: (i // nt, 0, i % nt, 0)),
        pl.BlockSpec((1, n_heads, cb, HEAD_DIM + V_PAD, MOBA_BLOCK), lambda i: (i // nt, 0, i % nt, 0, 0)),
        pl.BlockSpec((cb, 1, d_attn), lambda i: (i, 0, 0)),
    )
    return pl.pallas_call(
        functools.partial(_proj_prompt_kernel, d_attn=d_attn, d_pool=d_pool, n_heads=n_heads),
        out_shape=out_shape,
        grid=(t // tm,),
        in_specs=[pl.BlockSpec((tm, d_model), lambda i: (i, 0)),
                  _resident((d_model, d_in), lambda i: (0, 0))],
        out_specs=out_specs,
        compiler_params=_params(("arbitrary",)),
        name="proj_prompt",
    )(x2d, w_in)


def _attn_prompt_kernel(slope_ref, qt_ref, ka_ref, vt_ref, km_ref, o_ref, mask_ref, m_ref, acc_ref, s_ref):
    qi = pl.program_id(1)
    n_heads, nrow, tq = mask_ref.shape
    own = 2
    row = lax.broadcasted_iota(jnp.int32, (nrow, tq), 0).astype(F32)
    qif = jnp.full((1, tq), qi, jnp.int32).astype(F32)
    ik = lax.broadcasted_iota(jnp.int32, (MOBA_BLOCK, tq), 0)
    iq = lax.broadcasted_iota(jnp.int32, (MOBA_BLOCK, tq), 1)

    def scores(j, slot, heads=range(n_heads)):
        off = pl.multiple_of(j * MOBA_BLOCK, MOBA_BLOCK)
        for h in heads:
            s_ref[slot, h] = _dot(ka_ref[0, h, pl.ds(off, MOBA_BLOCK), :], qt_ref[0, h])

    def absorb(j, slot, heads=range(n_heads)):
        for h in heads:
            ps, alphas = [], []
            for c in range(tq // LANES):
                cols = slice(c * LANES, (c + 1) * LANES)
                s = s_ref[slot, h, :, cols]
                bias = mask_ref[h, pl.ds(j, 1), :][:, cols]
                m = m_ref[h, :, cols]
                m_new = jnp.maximum(m, jnp.max(s, axis=0, keepdims=True) + bias)
                alphas.append(jnp.exp2(m - m_new))
                ps.append(jnp.exp2(s - (m_new - bias)).astype(BF16))
                m_ref[h, :, cols] = m_new
            p = jnp.concatenate(ps, axis=1)
            acc_ref[h] = jnp.concatenate(alphas, axis=1) * acc_ref[h] + _dot(vt_ref[0, h, j], p)

    gates = [_dot(km_ref[0, h], qt_ref[0, h]) for h in range(n_heads)]
    for h in range(n_heads):
        scores(qi, own, [h])
        g = jnp.where(row < qif, gates[h], NEG_INF)
        sel = jnp.zeros((nrow, tq), jnp.bool_)
        for r in range(MOBA_TOPK):
            mx = jnp.max(g, axis=0, keepdims=True)
            idx = jnp.min(jnp.where(g == mx, row, float(nrow)), axis=0, keepdims=True)
            hit = row == idx
            sel = jnp.logical_or(sel, jnp.logical_and(hit, qif > float(r)))
            g = jnp.where(hit, -jnp.inf, g)
        mask_ref[h] = jnp.where(sel, slope_ref[h] * float(MOBA_BLOCK) * (row - qif), NEG_INF)

    for h in range(n_heads):
        scores(0, 0, [h])
        s = jnp.where(ik <= iq, s_ref[own, h], NEG_INF)
        m0 = jnp.max(s, axis=0, keepdims=True)
        p = jnp.exp2(s - m0)
        m_ref[h] = m0
        acc_ref[h] = _dot(vt_ref[0, h, qi], p.astype(BF16))

    for h in range(n_heads):
        scores(1, 1, [h])
        absorb(0, 0, [h])

    def body(jj, carry):
        j = 2 * jj
        for h in range(n_heads):
            scores(j, 0, [h])
            absorb(j - 1, 1, [h])
        for h in range(n_heads):
            scores(j + 1, 1, [h])
            absorb(j, 0, [h])
        return carry

    pairs = jnp.maximum((qi + 1) >> 1, 1)
    lax.fori_loop(1, pairs, body, 0)
    last = 2 * pairs - 1
    for h in range(n_heads):
        absorb(last, 1, [h])
        o_ref[0, h] = acc_ref[h, 0:HEAD_DIM, :] * (1.0 / acc_ref[h, HEAD_DIM:HEAD_DIM + 1, :])


def _attn_prompt(slopes, qt, ka, vt, km):
    batch, n_heads, _, seq = qt.shape
    nblk = seq // MOBA_BLOCK
    nrow = km.shape[2]
    per_batch = lambda nd: (lambda b, i: (b,) + (0,) * (nd - 1))
    return pl.pallas_call(
        _attn_prompt_kernel,
        out_shape=jax.ShapeDtypeStruct((batch, n_heads, HEAD_DIM, seq), F32),
        grid=(batch, nblk),
        in_specs=[
            pl.BlockSpec(slopes.shape, lambda b, i: (0, 0, 0)),
            pl.BlockSpec((1, n_heads, 2 * HEAD_DIM, MOBA_BLOCK), lambda b, i: (b, 0, 0, i)),
            _resident((1,) + ka.shape[1:], per_batch(4)),
            _resident((1,) + vt.shape[1:], per_batch(5)),
            pl.BlockSpec((1,) + km.shape[1:], per_batch(4)),
        ],
        out_specs=pl.BlockSpec((1, n_heads, HEAD_DIM, MOBA_BLOCK), lambda b, i: (b, 0, 0, i)),
        scratch_shapes=[pltpu.VMEM((n_heads, nrow, MOBA_BLOCK), F32),
                        pltpu.VMEM((n_heads, 1, MOBA_BLOCK), F32),
                        pltpu.VMEM((n_heads, HEAD_DIM + V_PAD, MOBA_BLOCK), F32),
                        pltpu.VMEM((3, n_heads, MOBA_BLOCK, MOBA_BLOCK), F32)],
        compiler_params=_params(("arbitrary", "arbitrary")),
        name="attn_prompt",
    )(slopes, qt, ka, vt, km)


def _tail(x, ya_b, pool_d, sg_ref, p_ref, wp_ref, ps_ref, wb_ref, wo_ref, g1_ref, b1_ref, wu_ref, wd_ref,
          g2_ref, b2_ref, wpe_ref, wpg_ref, o_ref, yb_ref, *, alpha, ff_chunk, side_work=(), after_mlp=None):
    d_model = x.shape[1]
    pool_ch = wp_ref.shape[1]
    br0 = _dot(ya_b, wb_ref[0])
    d = pool_d()
    for g in range(len(POOL_WINDOWS)):
        sl = slice(g * pool_ch, (g + 1) * pool_ch)
        yb = _dot(d[g].astype(BF16), wp_ref[g]) * ps_ref[:, sl]
        yb_ref[:, sl] = yb.astype(BF16)
    br1 = _dot(yb_ref[...], wb_ref[1])
    merged = sg_ref[:, 0:d_model].astype(F32) * br0 + sg_ref[:, d_model:].astype(F32) * br1
    mix = _dot(merged.astype(BF16), wo_ref[...])
    pe = _dot(p_ref[...].astype(BF16), wpe_ref[...])
    h1 = _layer_norm(alpha * x + mix, g1_ref[...], b1_ref[...])
    h1b = h1.astype(BF16)
    ple = _sigmoid(_dot(h1b, wpg_ref[...])) * pe
    acc = alpha * h1 + ple
    d_ff = wu_ref.shape[1]
    assert len(side_work) <= d_ff // ff_chunk
    for c in range(d_ff // ff_chunk):
        a = jnp.maximum(_dot(h1b, wu_ref[:, c * ff_chunk:(c + 1) * ff_chunk]), 0.0)
        if c < len(side_work):
            side_work[c]()
        acc = acc + _dot((a * a).astype(BF16), wd_ref[c * ff_chunk:(c + 1) * ff_chunk, :])
    if after_mlp is not None:
        after_mlp()
    o_ref[...] = _layer_norm(acc, g2_ref[...], b2_ref[...])


def _rank_sample_jobs(pages, n_pages, q_ref, g_ref, idx_ref, chunk, pages_per_block, nblk, n_jobs):
    blocks = n_pages // pages_per_block
    assert blocks % n_jobs == 0
    per_job = blocks // n_jobs

    def job(k):
        first = pages(k * per_job * pages_per_block)
        n_heads, head_dim, page_rows = first.shape
        lane = lax.broadcasted_iota(jnp.int32, (n_heads, LANES), 1)
        q = q_ref[0].reshape(n_heads, head_dim, 1)
        g = g_ref[...]
        if k == 0:
            g = jnp.where(chunk == 0, -jnp.inf, g)
        for r in range(k * per_job, (k + 1) * per_job):
            ksum = first if r == k * per_job else pages(r * pages_per_block)
            for s in range(1, pages_per_block):
                ksum = ksum + pages(r * pages_per_block + s)
            kmean = jnp.sum(ksum, axis=2, keepdims=True) * (1.0 / (page_rows * pages_per_block))
            gate = jnp.sum(kmean * q, axis=1)
            g = jnp.where(lane == chunk * blocks + r, gate, g)
        g_ref[...] = g
        if k < n_jobs - 1:
            return
        col = lane.astype(F32)
        out = jnp.zeros((n_heads, LANES), F32)
        for r in range(MOBA_TOPK):
            mx = jnp.max(g, axis=1, keepdims=True)
            idx = jnp.min(jnp.where(g == mx, col, float(LANES)), axis=1, keepdims=True)
            out = jnp.where(lane == r, idx, out)
            g = jnp.where(col == idx, -jnp.inf, g)
        idx_ref[0] = jnp.minimum(out.astype(jnp.int32), nblk - 1)

    return [functools.partial(job, k) for k in range(n_jobs)]


def _tail_prompt_kernel(pt_ref, x_ref, yt_ref, u_ref, halo_ref, sg_ref, p_ref, q_ref, ck_ref, *rest,
                        tiles_per_seq, chunks_per_sample, pages_per_block, nblk_sample, alpha, ff_chunk):
    *w_refs, o_ref, idx_ref, yb_ref, ue_ref, g_ref, page_buf, sem = rest
    i = pl.program_id(0)
    slot = i & 1
    n_pages = page_buf.shape[1]

    def page_copy(step, slot_, r):
        return pltpu.make_async_copy(ck_ref.at[pt_ref[step, r]], page_buf.at[slot_, r], sem.at[slot_])

    @pl.when(i == 0)
    def _():
        for r in range(n_pages):
            page_copy(0, 0, r).start()

    def fetch_next():
        @pl.when(i + 1 < pl.num_programs(0))
        def _():
            for r in range(n_pages):
                page_copy(i + 1, 1 - slot, r).start()

    blocks = n_pages // pages_per_block
    n_mlp_chunks = w_refs[6].shape[1] // ff_chunk
    n_jobs = max(j for j in range(1, n_mlp_chunks + 1) if blocks % j == 0)
    jobs = _rank_sample_jobs(lambda r: page_buf[slot, r], n_pages, q_ref, g_ref, idx_ref, i % chunks_per_sample,
                             pages_per_block, nblk_sample, n_jobs)

    def first_job():
        for r in range(n_pages):
            page_copy(i, slot, r).wait()
        jobs[0]()

    side_work = [first_job] + jobs[1:]

    tm = x_ref.shape[0]
    ti = pl.program_id(0) % tiles_per_seq

    def pool_d():
        u = u_ref[...]
        ue_ref[POOL_HALO:POOL_HALO + tm, :] = u
        ue_ref[0:POOL_HALO, :] = jnp.where(ti == 0, 0.0, halo_ref[...])
        pool_ch = u.shape[1] // len(POOL_WINDOWS)
        pos = ti * tm + lax.broadcasted_iota(jnp.int32, (tm, pool_ch), 0)
        d = []
        for g, w in enumerate(POOL_WINDOWS):
            sl = slice(g * pool_ch, (g + 1) * pool_ch)
            pooled = u[:, sl]
            for s in range(1, w):
                pooled = pooled + ue_ref[POOL_HALO - s:POOL_HALO - s + tm, sl]
            cnt = jnp.minimum(w, pos + 1).astype(F32)
            d.append(pooled / cnt - u[:, sl])
        return d

    ya_b = yt_ref[0].T.astype(BF16)
    _tail(x_ref[...], ya_b, pool_d, sg_ref, p_ref, *w_refs, o_ref, yb_ref, alpha=alpha, ff_chunk=ff_chunk,
          side_work=side_work, after_mlp=fetch_next)


def _tail_sample_kernel(x_ref, ya_ref, u_ref, st_ref, sg_ref, p_ref, *rest, cnts, alpha, ff_chunk):
    *w_refs, o_ref, yb_ref = rest
    def pool_d():
        u = u_ref[...]
        pool_ch = u.shape[1] // len(POOL_WINDOWS)
        d = []
        for g, w in enumerate(POOL_WINDOWS):
            pooled = u[:, g * pool_ch:(g + 1) * pool_ch]
            for s in range(1, w):
                pooled = pooled + st_ref[POOL_BUF - s][:, g * pool_ch:(g + 1) * pool_ch]
            d.append(pooled / cnts[g] - u[:, g * pool_ch:(g + 1) * pool_ch])
        return d

    _tail(x_ref[...], ya_ref[...].astype(BF16), pool_d, sg_ref, p_ref, *w_refs, o_ref, yb_ref,
          alpha=alpha, ff_chunk=ff_chunk)


def _weight_specs(weights, index_map_factory):
    return [_resident(w.shape, index_map_factory(w.ndim)) for w in weights]


def _tail_prompt(x2d, yt, u, sg, p2d, weights, seq, alpha, page_table, qcol, cache_kt, tm=256, ff_chunk=1024):
    t, d_model = x2d.shape
    d_pool = u.shape[1]
    batch = t // seq
    nt = seq // tm
    hb = tm // POOL_HALO
    steps = t // tm
    n_dec, n_pages = page_table.shape
    n_heads, head_dim, page_rows = cache_kt.shape[1:]
    ppb = MOBA_BLOCK // page_rows
    nblk = n_pages // ppb
    assert steps % n_dec == 0 and MOBA_TOPK <= nblk <= LANES
    cps = steps // n_dec
    assert n_pages % (cps * ppb) == 0
    pps = n_pages // cps
    yt3 = yt.reshape(batch, yt.shape[1] * yt.shape[2], seq)
    pages_by_step = page_table.reshape(steps, pps)
    in_specs = [
        pl.BlockSpec((tm, d_model), lambda i, pt: (i, 0)),
        pl.BlockSpec((1, yt3.shape[1], tm), lambda i, pt: (i // nt, 0, i % nt)),
        pl.BlockSpec((tm, d_pool), lambda i, pt: (i, 0)),
        pl.BlockSpec((POOL_HALO, d_pool), lambda i, pt: (jnp.maximum(i * hb - 1, 0), 0)),
        pl.BlockSpec((tm, sg.shape[1]), lambda i, pt: (i, 0)),
        pl.BlockSpec((tm, p2d.shape[1]), lambda i, pt: (i, 0)),
        pl.BlockSpec((1, n_heads * head_dim, 1), lambda i, pt: (i // cps, 0, 0)),
        pl.BlockSpec(memory_space=pl.ANY),
    ] + _weight_specs(weights, lambda nd: (lambda i, pt: (0,) * nd))
    grid_spec = pltpu.PrefetchScalarGridSpec(
        num_scalar_prefetch=1,
        grid=(steps,),
        in_specs=in_specs,
        out_specs=(pl.BlockSpec((tm, d_model), lambda i, pt: (i, 0)),
                   pl.BlockSpec((1, n_heads, LANES), lambda i, pt: (i // cps, 0, 0))),
        scratch_shapes=[pltpu.VMEM((tm, d_pool), BF16), pltpu.VMEM((POOL_HALO + tm, d_pool), F32),
                        pltpu.VMEM((n_heads, LANES), F32),
                        pltpu.VMEM((2, pps, n_heads, head_dim, page_rows), F32),
                        pltpu.SemaphoreType.DMA((2,))],
    )
    return pl.pallas_call(
        functools.partial(_tail_prompt_kernel, tiles_per_seq=nt, chunks_per_sample=cps,
                          pages_per_block=ppb, nblk_sample=nblk, alpha=alpha, ff_chunk=ff_chunk),
        out_shape=(jax.ShapeDtypeStruct((t, d_model), F32),
                   jax.ShapeDtypeStruct((n_dec, n_heads, LANES), jnp.int32)),
        grid_spec=grid_spec,
        compiler_params=_params(("arbitrary",)),
        name="tail_prompt",
    )(pages_by_step, x2d, yt3, u, u, sg, p2d, qcol, cache_kt, *weights)


def _tail_sample(x2d, ya, u, state_t, sg, p2d, weights, past_len, alpha, ff_chunk=1024):
    n, d_model = x2d.shape
    d_pool = u.shape[1]
    cnts = tuple(float(min(w, past_len + 1)) for w in POOL_WINDOWS)
    operands = (x2d, ya, u, state_t, sg, p2d)
    whole = lambda nd: (lambda i: (0,) * nd)
    in_specs = [pl.BlockSpec(a.shape, whole(a.ndim)) for a in operands] + _weight_specs(weights, whole)
    return pl.pallas_call(
        functools.partial(_tail_sample_kernel, cnts=cnts, alpha=alpha, ff_chunk=ff_chunk),
        out_shape=jax.ShapeDtypeStruct((n, d_model), F32),
        grid=(1,),
        in_specs=in_specs,
        out_specs=pl.BlockSpec((n, d_model), lambda i: (0, 0)),
        scratch_shapes=[pltpu.VMEM((n, d_pool), BF16)],
        compiler_params=_params(("arbitrary",)),
        name="tail_sample",
    )(*operands, *weights)


def _proj_sample_kernel(x_ref, w_ref, wq_ref, q_ref, k_ref, v_ref, u_ref, sg_ref, qg_ref, *, d_attn, d_pool):
    x = x_ref[...]
    z = _dot(x.astype(BF16), w_ref[...])
    q_ref[...] = z[:, 0:d_attn] * (HEAD_DIM ** -0.5)
    k_ref[...] = z[:, d_attn:2 * d_attn]
    v_ref[...] = z[:, 2 * d_attn:3 * d_attn]
    u_ref[...] = z[:, 3 * d_attn:3 * d_attn + d_pool]
    sg_ref[...] = _sigmoid(z[:, 3 * d_attn + d_pool:]).astype(BF16)
    qg_ref[...] = jnp.dot(x, wq_ref[...], precision=lax.Precision.HIGHEST, preferred_element_type=F32)


def _proj_sample(x2d, w_in, wq_f32):
    n, d_model = x2d.shape
    d_attn = d_model // 2
    d_pool = d_model // 2
    n_gate = w_in.shape[1] - 3 * d_attn - d_pool
    shapes = [(n, d_attn)] * 3 + [(n, d_pool)]
    whole = lambda s: pl.BlockSpec(s, lambda i: (0, 0))
    return pl.pallas_call(
        functools.partial(_proj_sample_kernel, d_attn=d_attn, d_pool=d_pool),
        out_shape=tuple(jax.ShapeDtypeStruct(s, F32) for s in shapes)
        + (jax.ShapeDtypeStruct((n, n_gate), BF16), jax.ShapeDtypeStruct((n, d_attn), F32)),
        grid=(1,),
        in_specs=[whole(x2d.shape), whole(w_in.shape), whole(wq_f32.shape)],
        out_specs=tuple(whole(s) for s in shapes) + (whole((n, n_gate)), whole((n, d_attn))),
        compiler_params=_params(("arbitrary",)),
        name="proj_sample",
    )(x2d, w_in, wq_f32)


def _attn_sample_kernel(pt_ref, sel_ref, slope_ref, q_ref, kn_ref, vn_ref, ck_ref, cv_ref, o_ref,
                        kbuf, vbuf, sem, *, pages_per_block, past_len):
    b = pl.program_id(0)
    n_seq = pl.num_programs(0)
    n_heads = q_ref.shape[1]
    head_dim, page_rows = kbuf.shape[2:]
    tiles_per_head = MOBA_TOPK * pages_per_block
    slot = b & 1

    def tile_copies(seq, slot_, h, t):
        r, s = divmod(t, pages_per_block)
        blk = sel_ref[(seq * n_heads + h) * MOBA_TOPK + r]
        page = pt_ref[seq, blk * pages_per_block + s]
        i = h * tiles_per_head + t
        return (pltpu.make_async_copy(ck_ref.at[page, h], kbuf.at[slot_, i], sem.at[slot_]),
                pltpu.make_async_copy(cv_ref.at[page, h], vbuf.at[slot_, i], sem.at[slot_]))

    def start_gather(seq, slot_):
        for h in range(n_heads):
            for t in range(tiles_per_head):
                for copy in tile_copies(seq, slot_, h, t):
                    copy.start()

    @pl.when(b == 0)
    def _():
        start_gather(0, 0)

    @pl.when(b + 1 < n_seq)
    def _():
        start_gather(b + 1, 1 - slot)

    for h in range(n_heads):
        for t in range(tiles_per_head):
            for copy in tile_copies(b, slot, h, t):
                copy.wait()

    kpos = lax.broadcasted_iota(jnp.int32, (1, page_rows), 1)
    for h in range(n_heads):
        q = q_ref[0, h].astype(BF16)
        q8 = jnp.broadcast_to(q, (8, head_dim))
        slope = slope_ref[h][:, 0:page_rows]
        s_self = jnp.sum(q.astype(F32) * kn_ref[0, h].astype(BF16).astype(F32), axis=1, keepdims=True)
        scores = []
        m = s_self
        for t in range(tiles_per_head):
            r, s = divmod(t, pages_per_block)
            blk = sel_ref[(b * n_heads + h) * MOBA_TOPK + r]
            kt = kbuf[slot, h * tiles_per_head + t].astype(BF16)
            dist = past_len - (blk * MOBA_BLOCK + s * page_rows + kpos)
            sc = _dot(q8, kt)[0:1, :] - slope * dist.astype(F32)
            scores.append(sc)
            m = jnp.maximum(m, jnp.max(sc, axis=1, keepdims=True))
        p_self = jnp.exp(s_self - m)
        l = p_self
        acc = p_self * vn_ref[0, h]
        for t, sc in enumerate(scores):
            p = jnp.exp(sc - m)
            l = l + jnp.sum(p, axis=1, keepdims=True)
            p8 = jnp.broadcast_to(p.astype(BF16), (8, page_rows))
            acc = acc + _dot_nt(p8, vbuf[slot, h * tiles_per_head + t].astype(BF16))[0:1, :]
        o_ref[0, h] = acc / l


def _attn_sample(page_table, sel, slopes, q4, k4, v4, cache_kt, cache_vt, past_len):
    n, n_heads, _, head_dim = q4.shape
    page_rows = cache_kt.shape[3]
    ppb = MOBA_BLOCK // page_rows
    n_tiles = n_heads * MOBA_TOPK * ppb
    tok = pl.BlockSpec((1, n_heads, 1, head_dim), lambda b, pt, sl: (b, 0, 0, 0))
    grid_spec = pltpu.PrefetchScalarGridSpec(
        num_scalar_prefetch=2,
        grid=(n,),
        in_specs=[pl.BlockSpec(slopes.shape, lambda b, pt, sl: (0, 0, 0)), tok, tok, tok,
                  pl.BlockSpec(memory_space=pl.ANY), pl.BlockSpec(memory_space=pl.ANY)],
        out_specs=tok,
        scratch_shapes=[pltpu.VMEM((2, n_tiles, head_dim, page_rows), F32),
                        pltpu.VMEM((2, n_tiles, head_dim, page_rows), F32),
                        pltpu.SemaphoreType.DMA((2,))],
    )
    return pl.pallas_call(
        functools.partial(_attn_sample_kernel, pages_per_block=ppb, past_len=past_len),
        out_shape=jax.ShapeDtypeStruct(q4.shape, F32),
        grid_spec=grid_spec,
        compiler_params=_params(("arbitrary",)),
        name="attn_sample",
    )(page_table, sel, slopes, q4, k4, v4, cache_kt, cache_vt)


def kernel(x_prompt, x_sample, cache_k, cache_v, state_pool, page_table, p_prompt, p_sample, w_in, w_branch,
           w_out, ln1_g, ln1_b, w_up, w_down, ln2_g, ln2_b, w_pe, w_pg, w_pool, pool_scale):
    depth = w_in.shape[0]
    batch, seq, d_model = x_prompt.shape
    n_dec, dec_seq, _ = x_sample.shape
    assert dec_seq == 1, "the sample path handles one new token per sequence"
    d_attn = d_model // 2
    d_pool = d_model // 2
    n_heads = d_attn // HEAD_DIM
    n_phys, page_rows = cache_k.shape[1:3]
    past_len = page_table.shape[1] * page_rows
    alpha = (2.0 * depth) ** 0.25
    nblk = seq // MOBA_BLOCK
    nrow = -(-nblk // 16) * 16
    assert past_len // MOBA_BLOCK >= MOBA_TOPK and page_rows <= LANES
    slopes = jnp.asarray([[[_alibi_slope(h, n_heads)] * MOBA_BLOCK] for h in range(n_heads)], F32)

    y_p = x_prompt.reshape(batch * seq, d_model)
    y_s = x_sample.reshape(n_dec, d_model)
    outs = [[] for _ in range(6)]
    for i in range(depth):
        w_in_b = w_in[i].astype(BF16)
        row = lambda a: a[i].reshape(1, -1)
        weights = (w_pool[i].astype(BF16), row(pool_scale), w_branch[i].astype(BF16), w_out[i].astype(BF16),
                   row(ln1_g), row(ln1_b), w_up[i].astype(BF16), w_down[i].astype(BF16),
                   row(ln2_g), row(ln2_b), w_pe[i].astype(BF16), w_pg[i].astype(BF16))

        qs, ks, vs, us, sgs, qg = _proj_sample(y_s, w_in_b, w_in[i][:, :d_attn])
        ckt = cache_k[i].transpose(0, 2, 3, 1)
        cvt = cache_v[i].transpose(0, 2, 3, 1)

        kt, vtf, u, sg, qt, ka, vt, km = _proj_prompt(y_p, w_in_b, batch, seq)
        km = km.reshape(batch, nblk, n_heads, HEAD_DIM).transpose(0, 2, 1, 3)
        km = jnp.pad(km, ((0, 0), (0, 0), (0, nrow - nblk), (0, HEAD_DIM))).astype(BF16)
        yt = _attn_prompt(slopes * LOG2E, qt, ka, vt, km)
        y_p, sel = _tail_prompt(y_p, yt, u, sg, p_prompt[i].reshape(batch * seq, -1), weights, seq, alpha,
                                page_table, qg.reshape(n_dec, d_attn, 1), ckt)
        outs[0].append(kt.transpose(0, 3, 1, 2))
        outs[1].append(vtf.transpose(0, 3, 1, 2))
        outs[2].append(u.reshape(batch, seq, d_pool)[:, seq - POOL_BUF:])

        q4, k4, v4 = (a.reshape(n_dec, n_heads, 1, HEAD_DIM) for a in (qs, ks, vs))
        sel = sel[:, :, :MOBA_TOPK].reshape(-1)
        ya_s = _attn_sample(page_table, sel, slopes, q4, k4, v4, ckt, cvt, past_len).reshape(n_dec, d_attn)
        state = state_pool[i]
        y_s = _tail_sample(y_s, ya_s, us, state.transpose(1, 0, 2), sgs,
                           p_sample[i].reshape(n_dec, -1), weights, past_len, alpha)
        outs[3].append(ks.reshape(n_dec, 1, n_heads, HEAD_DIM))
        outs[4].append(vs.reshape(n_dec, 1, n_heads, HEAD_DIM))
        outs[5].append(jnp.concatenate([state[:, 1:], us[:, None, :]], axis=1))

    k_p, v_p, pool_p, k_s, v_s, pool_s = (jnp.stack(o) for o in outs)
    return (y_p.reshape(batch, seq, d_model), y_s.reshape(n_dec, 1, d_model), k_p, v_p, pool_p, k_s, v_s, pool_s)
```

```python
import functools

import jax
import jax.numpy as jnp
import numpy as np
from jax import lax
from jax.experimental import pallas as pl
from jax.experimental.pallas import tpu as pltpu

HEAD_DIM = 64
MOBA_BLOCK = 256
MOBA_TOPK = 3
POOL_WINDOWS = (2, 4, 8, 16)
POOL_BUF = max(POOL_WINDOWS) - 1
POOL_HALO = 16
LN_EPS = 1e-5
NEG_INF = -1e30
LANES = 128
VMEM_LIMIT_BYTES = 56 * 1024 * 1024

F32 = jnp.float32
BF16 = jnp.bfloat16


def _dot(a, b):
    return jnp.dot(a, b, preferred_element_type=F32)


def _dot_nt(a, b):
    return lax.dot_general(a, b, (((1,), (1,)), ((), ())), preferred_element_type=F32)


def _resident(block_shape, index_map):
    return pl.BlockSpec(block_shape, index_map, pipeline_mode=pl.Buffered(1))


def _params(semantics):
    return pltpu.CompilerParams(dimension_semantics=semantics, vmem_limit_bytes=VMEM_LIMIT_BYTES)


def _sigmoid(x):
    return 1.0 / (1.0 + jnp.exp(-x))


def _layer_norm(x, g, b):
    mu = jnp.mean(x, axis=-1, keepdims=True)
    xc = x - mu
    var = jnp.mean(xc * xc, axis=-1, keepdims=True)
    return xc * lax.rsqrt(var + LN_EPS) * g + b


LOG2E = 1.4426950408889634
V_PAD = 16


def _alibi_slope(h, n_heads):
    return 2.0 ** (-8.0 * (h + 1) / n_heads)


def _split_bf16(c):
    hi = float(np.asarray(c, dtype=BF16))
    lo = float(np.asarray(c - hi, dtype=BF16))
    return hi, lo


def _proj_prompt_kernel(x_ref, w_ref, k_ref, v_ref, u_ref, sg_ref, qt_ref, ka_ref, vt_ref, km_ref,
                        *, d_attn, d_pool, n_heads):
    tm = x_ref.shape[0]
    xb = x_ref[...].astype(BF16)
    o_k, o_v, o_u, o_g = d_attn, 2 * d_attn, 3 * d_attn, 3 * d_attn + d_pool

    q = _dot(xb, w_ref[:, 0:o_k]) * (LOG2E * HEAD_DIM ** -0.5)
    qt = q.T
    row = lax.broadcasted_iota(jnp.int32, (HEAD_DIM, tm), 0)
    for h in range(n_heads):
        c_hi, c_lo = _split_bf16(LOG2E * _alibi_slope(h, n_heads))
        qt_ref[0, h, 0:HEAD_DIM, :] = qt[h * HEAD_DIM:(h + 1) * HEAD_DIM, :].astype(BF16)
        extra = jnp.where(row == 0, 16.0 * c_hi, jnp.where(row == 1, c_hi,
                          jnp.where(row == 2, 16.0 * c_lo, jnp.where(row == 3, c_lo, 0.0))))
        qt_ref[0, h, HEAD_DIM:2 * HEAD_DIM, :] = extra.astype(BF16)

    k = _dot(xb, w_ref[:, o_k:o_v])
    kt = k.T
    for h in range(n_heads):
        k_ref[0, h] = kt[h * HEAD_DIM:(h + 1) * HEAD_DIM, :]
    for c in range(tm // MOBA_BLOCK):
        blk = k[c * MOBA_BLOCK:(c + 1) * MOBA_BLOCK, :]
        km_ref[c] = jnp.sum(blk, axis=0, keepdims=True) * (1.0 / MOBA_BLOCK)
    prow = lax.broadcasted_iota(jnp.int32, (tm, LANES), 0) & (MOBA_BLOCK - 1)
    lane = lax.broadcasted_iota(jnp.int32, (tm, LANES), 1)
    sixteens = jnp.logical_or(lane == HEAD_DIM, lane == HEAD_DIM + 2)
    units = jnp.logical_or(lane == HEAD_DIM + 1, lane == HEAD_DIM + 3)
    posf = jnp.where(sixteens, (prow >> 4).astype(F32), jnp.where(units, (prow & 15).astype(F32), 0.0))
    low = lane < HEAD_DIM
    for m in range(n_heads // 2):
        col = k[:, m * LANES:(m + 1) * LANES]
        ka_ref[0, 2 * m] = jnp.where(low, col, posf).astype(BF16)
        ka_ref[0, 2 * m + 1] = jnp.where(low, pltpu.roll(col, HEAD_DIM, 1), posf).astype(BF16)

    vt = _dot(xb, w_ref[:, o_v:o_u]).T
    ones_row = jnp.where(lax.broadcasted_iota(jnp.int32, (V_PAD, MOBA_BLOCK), 0) == 0, 1.0, 0.0).astype(BF16)
    for h in range(n_heads):
        v_ref[0, h] = vt[h * HEAD_DIM:(h + 1) * HEAD_DIM, :]
        for c in range(tm // MOBA_BLOCK):
            vt_ref[0, h, c, 0:HEAD_DIM, :] = vt[h * HEAD_DIM:(h + 1) * HEAD_DIM,
                                                c * MOBA_BLOCK:(c + 1) * MOBA_BLOCK].astype(BF16)
            vt_ref[0, h, c, HEAD_DIM:HEAD_DIM + V_PAD, :] = ones_row

    u_ref[...] = _dot(xb, w_ref[:, o_u:o_g])
    sg_ref[...] = _sigmoid(_dot(xb, w_ref[:, o_g:])).astype(BF16)


def _proj_prompt(x2d, w_in, batch, seq, tm=512):
    t, d_model = x2d.shape
    d_attn = d_model // 2
    d_pool = d_model // 2
    n_heads = d_attn // HEAD_DIM
    d_in = w_in.shape[1]
    n_gate = d_in - 3 * d_attn - d_pool
    nt = seq // tm
    nblk = seq // MOBA_BLOCK
    cb = tm // MOBA_BLOCK
    out_shape = (
        jax.ShapeDtypeStruct((batch, n_heads, HEAD_DIM, seq), F32),
        jax.ShapeDtypeStruct((batch, n_heads, HEAD_DIM, seq), F32),
        jax.ShapeDtypeStruct((t, d_pool), F32),
        jax.ShapeDtypeStruct((t, n_gate), BF16),
        jax.ShapeDtypeStruct((batch, n_heads, 2 * HEAD_DIM, seq), BF16),
        jax.ShapeDtypeStruct((batch, n_heads, seq, 2 * HEAD_DIM), BF16),
        jax.ShapeDtypeStruct((batch, n_heads, nblk, HEAD_DIM + V_PAD, MOBA_BLOCK), BF16),
        jax.ShapeDtypeStruct((t // MOBA_BLOCK, 1, d_attn), F32),
    )
    out_specs = (
        pl.BlockSpec((1, n_heads, HEAD_DIM, tm), lambda i: (i // nt, 0, 0, i % nt)),
        pl.BlockSpec((1, n_heads, HEAD_DIM, tm), lambda i: (i // nt, 0, 0, i % nt)),
        pl.BlockSpec((tm, d_pool), lambda i: (i, 0)),
        pl.BlockSpec((tm, n_gate), lambda i: (i, 0)),
        pl.BlockSpec((1, n_heads, 2 * HEAD_DIM, tm), lambda i: (i // nt, 0, 0, i % nt)),
        pl.BlockSpec((1, n_heads, tm, 2 * HEAD_DIM), lambda i: (i // nt, 0, i % nt, 0)),
        pl.BlockSpec((1, n_heads, cb, HEAD_DIM + V_PAD, MOBA_BLOCK), lambda i: (i // nt, 0, i % nt, 0, 0)),
        pl.BlockSpec((cb, 1, d_attn), lambda i: (i, 0, 0)),
    )
    return pl.pallas_call(
        functools.partial(_proj_prompt_kernel, d_attn=d_attn, d_pool=d_pool, n_heads=n_heads),
        out_shape=out_shape,
        grid=(t // tm,),
        in_specs=[pl.BlockSpec((tm, d_model), lambda i: (i, 0)),
                  _resident((d_model, d_in), lambda i: (0, 0))],
        out_specs=out_specs,
        compiler_params=_params(("arbitrary",)),
        name="proj_prompt",
    )(x2d, w_in)


def _attn_prompt_kernel(slope_ref, qt_ref, ka_ref, vt_ref, km_ref, o_ref, mask_ref, m_ref, acc_ref, s_ref):
    n_tiles = mask_ref.shape[0]
    for t in range(n_tiles):
        _attn_prompt_tile(pl.program_id(1) * n_tiles + t, slice(t * MOBA_BLOCK, (t + 1) * MOBA_BLOCK),
                          slope_ref, qt_ref, ka_ref, vt_ref, km_ref, o_ref,
                          mask_ref.at[t], m_ref.at[t], acc_ref.at[t], s_ref.at[t])


def _attn_prompt_tile(qi, qcols, slope_ref, qt_ref, ka_ref, vt_ref, km_ref, o_ref, mask_ref, m_ref, acc_ref, s_ref):
    n_heads, nrow, tq = mask_ref.shape
    own = 2
    row = lax.broadcasted_iota(jnp.int32, (nrow, tq), 0).astype(F32)
    qif = jnp.full((1, tq), qi, jnp.int32).astype(F32)
    ik = lax.broadcasted_iota(jnp.int32, (MOBA_BLOCK, tq), 0)
    iq = lax.broadcasted_iota(jnp.int32, (MOBA_BLOCK, tq), 1)

    def scores(j, slot, heads=range(n_heads)):
        off = pl.multiple_of(j * MOBA_BLOCK, MOBA_BLOCK)
        for h in heads:
            s_ref[slot, h] = _dot(ka_ref[0, h, pl.ds(off, MOBA_BLOCK), :], qt_ref[0, h, :, qcols])

    def absorb(j, slot, heads=range(n_heads)):
        for h in heads:
            ps, alphas = [], []
            for c in range(tq // LANES):
                cols = slice(c * LANES, (c + 1) * LANES)
                s = s_ref[slot, h, :, cols]
                bias = mask_ref[h, pl.ds(j, 1), :][:, cols]
                m = m_ref[h, :, cols]
                m_new = jnp.maximum(m, jnp.max(s, axis=0, keepdims=True) + bias)
                alphas.append(jnp.exp2(m - m_new))
                ps.append(jnp.exp2(s - (m_new - bias)).astype(BF16))
                m_ref[h, :, cols] = m_new
            p = jnp.concatenate(ps, axis=1)
            acc_ref[h] = jnp.concatenate(alphas, axis=1) * acc_ref[h] + _dot(vt_ref[0, h, j], p)

    gates = [_dot(km_ref[0, h], qt_ref[0, h, :, qcols]) for h in range(n_heads)]
    for h in range(n_heads):
        scores(qi, own, [h])
        g = jnp.where(row < qif, gates[h], NEG_INF)
        sel = jnp.zeros((nrow, tq), jnp.bool_)
        for r in range(MOBA_TOPK):
            mx = jnp.max(g, axis=0, keepdims=True)
            idx = jnp.min(jnp.where(g == mx, row, float(nrow)), axis=0, keepdims=True)
            hit = row == idx
            sel = jnp.logical_or(sel, jnp.logical_and(hit, qif > float(r)))
            g = jnp.where(hit, -jnp.inf, g)
        mask_ref[h] = jnp.where(sel, slope_ref[h] * float(MOBA_BLOCK) * (row - qif), NEG_INF)

    for h in range(n_heads):
        scores(0, 0, [h])
        s = jnp.where(ik <= iq, s_ref[own, h], NEG_INF)
        m0 = jnp.max(s, axis=0, keepdims=True)
        p = jnp.exp2(s - m0)
        m_ref[h] = m0
        acc_ref[h] = _dot(vt_ref[0, h, qi], p.astype(BF16))

    for h in range(n_heads):
        scores(1, 1, [h])
        absorb(0, 0, [h])

    def body(jj, carry):
        j = 2 * jj
        for h in range(n_heads):
            scores(j, 0, [h])
            absorb(j - 1, 1, [h])
        for h in range(n_heads):
            scores(j + 1, 1, [h])
            absorb(j, 0, [h])
        return carry

    pairs = jnp.maximum((qi + 1) >> 1, 1)
    lax.fori_loop(1, pairs, body, 0)
    last = 2 * pairs - 1
    for h in range(n_heads):
        absorb(last, 1, [h])
        o_ref[0, h, :, qcols] = acc_ref[h, 0:HEAD_DIM, :] * (1.0 / acc_ref[h, HEAD_DIM:HEAD_DIM + 1, :])


def _attn_prompt(slopes, qt, ka, vt, km, tiles=2):
    batch, n_heads, _, seq = qt.shape
    nblk = seq // MOBA_BLOCK
    nrow = km.shape[2]
    assert nblk % tiles == 0
    tq = tiles * MOBA_BLOCK
    per_batch = lambda nd: (lambda b, i: (b,) + (0,) * (nd - 1))
    return pl.pallas_call(
        _attn_prompt_kernel,
        out_shape=jax.ShapeDtypeStruct((batch, n_heads, HEAD_DIM, seq), F32),
        grid=(batch, nblk // tiles),
        in_specs=[
            pl.BlockSpec(slopes.shape, lambda b, i: (0, 0, 0)),
            pl.BlockSpec((1, n_heads, 2 * HEAD_DIM, tq), lambda b, i: (b, 0, 0, i)),
            _resident((1,) + ka.shape[1:], per_batch(4)),
            _resident((1,) + vt.shape[1:], per_batch(5)),
            pl.BlockSpec((1,) + km.shape[1:], per_batch(4)),
        ],
        out_specs=pl.BlockSpec((1, n_heads, HEAD_DIM, tq), lambda b, i: (b, 0, 0, i)),
        scratch_shapes=[pltpu.VMEM((tiles, n_heads, nrow, MOBA_BLOCK), F32),
                        pltpu.VMEM((tiles, n_heads, 1, MOBA_BLOCK), F32),
                        pltpu.VMEM((tiles, n_heads, HEAD_DIM + V_PAD, MOBA_BLOCK), F32),
                        pltpu.VMEM((tiles, 3, n_heads, MOBA_BLOCK, MOBA_BLOCK), F32)],
        compiler_params=_params(("arbitrary", "arbitrary")),
        name="attn_prompt",
    )(slopes, qt, ka, vt, km)


def _tail(x, ya_b, pool_d, sg_ref, p_ref, wp_ref, ps_ref, wb_ref, wo_ref, g1_ref, b1_ref, wu_ref, wd_ref,
          g2_ref, b2_ref, wpe_ref, wpg_ref, o_ref, yb_ref, *, alpha, ff_chunk, side_work=(), after_mlp=None):
    d_model = x.shape[1]
    pool_ch = wp_ref.shape[1]
    br0 = _dot(ya_b, wb_ref[0])
    d = pool_d()
    for g in range(len(POOL_WINDOWS)):
        sl = slice(g * pool_ch, (g + 1) * pool_ch)
        yb = _dot(d[g].astype(BF16), wp_ref[g]) * ps_ref[:, sl]
        yb_ref[:, sl] = yb.astype(BF16)
    br1 = _dot(yb_ref[...], wb_ref[1])
    merged = sg_ref[:, 0:d_model].astype(F32) * br0 + sg_ref[:, d_model:].astype(F32) * br1
    mix = _dot(merged.astype(BF16), wo_ref[...])
    pe = _dot(p_ref[...].astype(BF16), wpe_ref[...])
    h1 = _layer_norm(alpha * x + mix, g1_ref[...], b1_ref[...])
    h1b = h1.astype(BF16)
    ple = _sigmoid(_dot(h1b, wpg_ref[...])) * pe
    acc = alpha * h1 + ple
    d_ff = wu_ref.shape[1]
    assert len(side_work) <= d_ff // ff_chunk
    for c in range(d_ff // ff_chunk):
        a = jnp.maximum(_dot(h1b, wu_ref[:, c * ff_chunk:(c + 1) * ff_chunk]), 0.0)
        if c < len(side_work):
            side_work[c]()
        acc = acc + _dot((a * a).astype(BF16), wd_ref[c * ff_chunk:(c + 1) * ff_chunk, :])
    if after_mlp is not None:
        after_mlp()
    o_ref[...] = _layer_norm(acc, g2_ref[...], b2_ref[...])


def _rank_sample_jobs(pages, n_pages, q_ref, g_ref, idx_ref, chunk, pages_per_block, nblk, n_jobs):
    blocks = n_pages // pages_per_block
    assert blocks % n_jobs == 0
    per_job = blocks // n_jobs

    def job(k):
        first = pages(k * per_job * pages_per_block)
        n_heads, head_dim, page_rows = first.shape
        lane = lax.broadcasted_iota(jnp.int32, (n_heads, LANES), 1)
        q = q_ref[0].reshape(n_heads, head_dim, 1)
        g = g_ref[...]
        if k == 0:
            g = jnp.where(chunk == 0, -jnp.inf, g)
        for r in range(k * per_job, (k + 1) * per_job):
            ksum = first if r == k * per_job else pages(r * pages_per_block)
            for s in range(1, pages_per_block):
                ksum = ksum + pages(r * pages_per_block + s)
            kmean = jnp.sum(ksum, axis=2, keepdims=True) * (1.0 / (page_rows * pages_per_block))
            gate = jnp.sum(kmean * q, axis=1)
            g = jnp.where(lane == chunk * blocks + r, gate, g)
        g_ref[...] = g
        if k < n_jobs - 1:
            return
        col = lane.astype(F32)
        out = jnp.zeros((n_heads, LANES), F32)
        for r in range(MOBA_TOPK):
            mx = jnp.max(g, axis=1, keepdims=True)
            idx = jnp.min(jnp.where(g == mx, col, float(LANES)), axis=1, keepdims=True)
            out = jnp.where(lane == r, idx, out)
            g = jnp.where(col == idx, -jnp.inf, g)
        idx_ref[0] = jnp.minimum(out.astype(jnp.int32), nblk - 1)

    return [functools.partial(job, k) for k in range(n_jobs)]


def _tail_prompt_kernel(pt_ref, x_ref, yt_ref, u_ref, halo_ref, sg_ref, p_ref, q_ref, ck_ref, *rest,
                        tiles_per_seq, chunks_per_sample, pages_per_block, nblk_sample, alpha, ff_chunk):
    *w_refs, o_ref, idx_ref, yb_ref, ue_ref, g_ref, page_buf, sem = rest
    i = pl.program_id(0)
    slot = i & 1
    n_pages = page_buf.shape[1]

    def page_copy(step, slot_, r):
        return pltpu.make_async_copy(ck_ref.at[pt_ref[step, r]], page_buf.at[slot_, r], sem.at[slot_])

    @pl.when(i == 0)
    def _():
        for r in range(n_pages):
            page_copy(0, 0, r).start()

    def fetch_next():
        @pl.when(i + 1 < pl.num_programs(0))
        def _():
            for r in range(n_pages):
                page_copy(i + 1, 1 - slot, r).start()

    blocks = n_pages // pages_per_block
    n_mlp_chunks = w_refs[6].shape[1] // ff_chunk
    n_jobs = max(j for j in range(1, n_mlp_chunks + 1) if blocks % j == 0)
    jobs = _rank_sample_jobs(lambda r: page_buf[slot, r], n_pages, q_ref, g_ref, idx_ref, i % chunks_per_sample,
                             pages_per_block, nblk_sample, n_jobs)

    def first_job():
        for r in range(n_pages):
            page_copy(i, slot, r).wait()
        jobs[0]()

    side_work = [first_job] + jobs[1:]

    tm = x_ref.shape[0]
    ti = pl.program_id(0) % tiles_per_seq

    def pool_d():
        u = u_ref[...]
        ue_ref[POOL_HALO:POOL_HALO + tm, :] = u
        ue_ref[0:POOL_HALO, :] = jnp.where(ti == 0, 0.0, halo_ref[...])
        pool_ch = u.shape[1] // len(POOL_WINDOWS)
        pos = ti * tm + lax.broadcasted_iota(jnp.int32, (tm, pool_ch), 0)
        d = []
        for g, w in enumerate(POOL_WINDOWS):
            sl = slice(g * pool_ch, (g + 1) * pool_ch)
            pooled = u[:, sl]
            for s in range(1, w):
                pooled = pooled + ue_ref[POOL_HALO - s:POOL_HALO - s + tm, sl]
            cnt = jnp.minimum(w, pos + 1).astype(F32)
            d.append(pooled / cnt - u[:, sl])
        return d

    ya_b = yt_ref[0].T.astype(BF16)
    _tail(x_ref[...], ya_b, pool_d, sg_ref, p_ref, *w_refs, o_ref, yb_ref, alpha=alpha, ff_chunk=ff_chunk,
          side_work=side_work, after_mlp=fetch_next)


def _tail_sample_kernel(x_ref, ya_ref, u_ref, st_ref, sg_ref, p_ref, *rest, cnts, alpha, ff_chunk):
    *w_refs, o_ref, yb_ref = rest
    def pool_d():
        u = u_ref[...]
        pool_ch = u.shape[1] // len(POOL_WINDOWS)
        d = []
        for g, w in enumerate(POOL_WINDOWS):
            pooled = u[:, g * pool_ch:(g + 1) * pool_ch]
            for s in range(1, w):
                pooled = pooled + st_ref[POOL_BUF - s][:, g * pool_ch:(g + 1) * pool_ch]
            d.append(pooled / cnts[g] - u[:, g * pool_ch:(g + 1) * pool_ch])
        return d

    _tail(x_ref[...], ya_ref[...].astype(BF16), pool_d, sg_ref, p_ref, *w_refs, o_ref, yb_ref,
          alpha=alpha, ff_chunk=ff_chunk)


def _weight_specs(weights, index_map_factory):
    return [_resident(w.shape, index_map_factory(w.ndim)) for w in weights]


def _tail_prompt(x2d, yt, u, sg, p2d, weights, seq, alpha, page_table, qcol, cache_kt, tm=256, ff_chunk=1024):
    t, d_model = x2d.shape
    d_pool = u.shape[1]
    batch = t // seq
    nt = seq // tm
    hb = tm // POOL_HALO
    steps = t // tm
    n_dec, n_pages = page_table.shape
    n_heads, head_dim, page_rows = cache_kt.shape[1:]
    ppb = MOBA_BLOCK // page_rows
    nblk = n_pages // ppb
    assert steps % n_dec == 0 and MOBA_TOPK <= nblk <= LANES
    cps = steps // n_dec
    assert n_pages % (cps * ppb) == 0
    pps = n_pages // cps
    yt3 = yt.reshape(batch, yt.shape[1] * yt.shape[2], seq)
    pages_by_step = page_table.reshape(steps, pps)
    in_specs = [
        pl.BlockSpec((tm, d_model), lambda i, pt: (i, 0)),
        pl.BlockSpec((1, yt3.shape[1], tm), lambda i, pt: (i // nt, 0, i % nt)),
        pl.BlockSpec((tm, d_pool), lambda i, pt: (i, 0)),
        pl.BlockSpec((POOL_HALO, d_pool), lambda i, pt: (jnp.maximum(i * hb - 1, 0), 0)),
        pl.BlockSpec((tm, sg.shape[1]), lambda i, pt: (i, 0)),
        pl.BlockSpec((tm, p2d.shape[1]), lambda i, pt: (i, 0)),
        pl.BlockSpec((1, n_heads * head_dim, 1), lambda i, pt: (i // cps, 0, 0)),
        pl.BlockSpec(memory_space=pl.ANY),
    ] + _weight_specs(weights, lambda nd: (lambda i, pt: (0,) * nd))
    grid_spec = pltpu.PrefetchScalarGridSpec(
        num_scalar_prefetch=1,
        grid=(steps,),
        in_specs=in_specs,
        out_specs=(pl.BlockSpec((tm, d_model), lambda i, pt: (i, 0)),
                   pl.BlockSpec((1, n_heads, LANES), lambda i, pt: (i // cps, 0, 0))),
        scratch_shapes=[pltpu.VMEM((tm, d_pool), BF16), pltpu.VMEM((POOL_HALO + tm, d_pool), F32),
                        pltpu.VMEM((n_heads, LANES), F32),
                        pltpu.VMEM((2, pps, n_heads, head_dim, page_rows), F32),
                        pltpu.SemaphoreType.DMA((2,))],
    )
    return pl.pallas_call(
        functools.partial(_tail_prompt_kernel, tiles_per_seq=nt, chunks_per_sample=cps,
                          pages_per_block=ppb, nblk_sample=nblk, alpha=alpha, ff_chunk=ff_chunk),
        out_shape=(jax.ShapeDtypeStruct((t, d_model), F32),
                   jax.ShapeDtypeStruct((n_dec, n_heads, LANES), jnp.int32)),
        grid_spec=grid_spec,
        compiler_params=_params(("arbitrary",)),
        name="tail_prompt",
    )(pages_by_step, x2d, yt3, u, u, sg, p2d, qcol, cache_kt, *weights)


def _tail_sample(x2d, ya, u, state_t, sg, p2d, weights, past_len, alpha, ff_chunk=1024):
    n, d_model = x2d.shape
    d_pool = u.shape[1]
    cnts = tuple(float(min(w, past_len + 1)) for w in POOL_WINDOWS)
    operands = (x2d, ya, u, state_t, sg, p2d)
    whole = lambda nd: (lambda i: (0,) * nd)
    in_specs = [pl.BlockSpec(a.shape, whole(a.ndim)) for a in operands] + _weight_specs(weights, whole)
    return pl.pallas_call(
        functools.partial(_tail_sample_kernel, cnts=cnts, alpha=alpha, ff_chunk=ff_chunk),
        out_shape=jax.ShapeDtypeStruct((n, d_model), F32),
        grid=(1,),
        in_specs=in_specs,
        out_specs=pl.BlockSpec((n, d_model), lambda i: (0, 0)),
        scratch_shapes=[pltpu.VMEM((n, d_pool), BF16)],
        compiler_params=_params(("arbitrary",)),
        name="tail_sample",
    )(*operands, *weights)


def _proj_sample_kernel(x_ref, w_ref, wq_ref, q_ref, k_ref, v_ref, u_ref, sg_ref, qg_ref, *, d_attn, d_pool):
    x = x_ref[...]
    z = _dot(x.astype(BF16), w_ref[...])
    q_ref[...] = z[:, 0:d_attn] * (HEAD_DIM ** -0.5)
    k_ref[...] = z[:, d_attn:2 * d_attn]
    v_ref[...] = z[:, 2 * d_attn:3 * d_attn]
    u_ref[...] = z[:, 3 * d_attn:3 * d_attn + d_pool]
    sg_ref[...] = _sigmoid(z[:, 3 * d_attn + d_pool:]).astype(BF16)
    qg_ref[...] = jnp.dot(x, wq_ref[...], precision=lax.Precision.HIGHEST, preferred_element_type=F32)


def _proj_sample(x2d, w_in, w_in_f32):
    n, d_model = x2d.shape
    d_attn = d_model // 2
    d_pool = d_model // 2
    n_gate = w_in.shape[1] - 3 * d_attn - d_pool
    shapes = [(n, d_attn)] * 3 + [(n, d_pool)]
    whole = lambda s: pl.BlockSpec(s, lambda i: (0, 0))
    return pl.pallas_call(
        functools.partial(_proj_sample_kernel, d_attn=d_attn, d_pool=d_pool),
        out_shape=tuple(jax.ShapeDtypeStruct(s, F32) for s in shapes)
        + (jax.ShapeDtypeStruct((n, n_gate), BF16), jax.ShapeDtypeStruct((n, d_attn), F32)),
        grid=(1,),
        in_specs=[whole(x2d.shape), whole(w_in.shape), whole((d_model, d_attn))],
        out_specs=tuple(whole(s) for s in shapes) + (whole((n, n_gate)), whole((n, d_attn))),
        compiler_params=_params(("arbitrary",)),
        name="proj_sample",
    )(x2d, w_in, w_in_f32)


def _attn_sample_kernel(pt_ref, sel_ref, slope_ref, q_ref, kn_ref, vn_ref, ck_ref, cv_ref, o_ref,
                        kbuf, vbuf, sem, *, pages_per_block, past_len):
    b = pl.program_id(0)
    n_seq = pl.num_programs(0)
    n_heads = q_ref.shape[1]
    head_dim, page_rows = kbuf.shape[2:]
    tiles_per_head = MOBA_TOPK * pages_per_block
    slot = b & 1

    def tile_copies(seq, slot_, h, t):
        r, s = divmod(t, pages_per_block)
        blk = sel_ref[(seq * n_heads + h) * MOBA_TOPK + r]
        page = pt_ref[seq, blk * pages_per_block + s]
        i = h * tiles_per_head + t
        return (pltpu.make_async_copy(ck_ref.at[page, h], kbuf.at[slot_, i], sem.at[slot_]),
                pltpu.make_async_copy(cv_ref.at[page, h], vbuf.at[slot_, i], sem.at[slot_]))

    def start_gather(seq, slot_):
        for h in range(n_heads):
            for t in range(tiles_per_head):
                for copy in tile_copies(seq, slot_, h, t):
                    copy.start()

    @pl.when(b == 0)
    def _():
        start_gather(0, 0)

    @pl.when(b + 1 < n_seq)
    def _():
        start_gather(b + 1, 1 - slot)

    for h in range(n_heads):
        for t in range(tiles_per_head):
            for copy in tile_copies(b, slot, h, t):
                copy.wait()

    kpos = lax.broadcasted_iota(jnp.int32, (1, page_rows), 1)
    for h in range(n_heads):
        q = q_ref[0, h].astype(BF16)
        q8 = jnp.broadcast_to(q, (8, head_dim))
        slope = slope_ref[h][:, 0:page_rows]
        s_self = jnp.sum(q.astype(F32) * kn_ref[0, h].astype(BF16).astype(F32), axis=1, keepdims=True)
        scores = []
        m = s_self
        for t in range(tiles_per_head):
            r, s = divmod(t, pages_per_block)
            blk = sel_ref[(b * n_heads + h) * MOBA_TOPK + r]
            kt = kbuf[slot, h * tiles_per_head + t].astype(BF16)
            dist = past_len - (blk * MOBA_BLOCK + s * page_rows + kpos)
            sc = _dot(q8, kt)[0:1, :] - slope * dist.astype(F32)
            scores.append(sc)
            m = jnp.maximum(m, jnp.max(sc, axis=1, keepdims=True))
        p_self = jnp.exp(s_self - m)
        l = p_self
        acc = p_self * vn_ref[0, h]
        for t, sc in enumerate(scores):
            p = jnp.exp(sc - m)
            l = l + jnp.sum(p, axis=1, keepdims=True)
            p8 = jnp.broadcast_to(p.astype(BF16), (8, page_rows))
            acc = acc + _dot_nt(p8, vbuf[slot, h * tiles_per_head + t].astype(BF16))[0:1, :]
        o_ref[0, h] = acc / l


def _attn_sample(page_table, sel, slopes, q4, k4, v4, cache_kt, cache_vt, past_len):
    n, n_heads, _, head_dim = q4.shape
    page_rows = cache_kt.shape[3]
    ppb = MOBA_BLOCK // page_rows
    n_tiles = n_heads * MOBA_TOPK * ppb
    tok = pl.BlockSpec((1, n_heads, 1, head_dim), lambda b, pt, sl: (b, 0, 0, 0))
    grid_spec = pltpu.PrefetchScalarGridSpec(
        num_scalar_prefetch=2,
        grid=(n,),
        in_specs=[pl.BlockSpec(slopes.shape, lambda b, pt, sl: (0, 0, 0)), tok, tok, tok,
                  pl.BlockSpec(memory_space=pl.ANY), pl.BlockSpec(memory_space=pl.ANY)],
        out_specs=tok,
        scratch_shapes=[pltpu.VMEM((2, n_tiles, head_dim, page_rows), F32),
                        pltpu.VMEM((2, n_tiles, head_dim, page_rows), F32),
                        pltpu.SemaphoreType.DMA((2,))],
    )
    return pl.pallas_call(
        functools.partial(_attn_sample_kernel, pages_per_block=ppb, past_len=past_len),
        out_shape=jax.ShapeDtypeStruct(q4.shape, F32),
        grid_spec=grid_spec,
        compiler_params=_params(("arbitrary",)),
        name="attn_sample",
    )(page_table, sel, slopes, q4, k4, v4, cache_kt, cache_vt)


def kernel(x_prompt, x_sample, cache_k, cache_v, state_pool, page_table, p_prompt, p_sample, w_in, w_branch,
           w_out, ln1_g, ln1_b, w_up, w_down, ln2_g, ln2_b, w_pe, w_pg, w_pool, pool_scale):
    depth = w_in.shape[0]
    batch, seq, d_model = x_prompt.shape
    n_dec, dec_seq, _ = x_sample.shape
    assert dec_seq == 1, "the sample path handles one new token per sequence"
    d_attn = d_model // 2
    d_pool = d_model // 2
    n_heads = d_attn // HEAD_DIM
    n_phys, page_rows = cache_k.shape[1:3]
    past_len = page_table.shape[1] * page_rows
    alpha = (2.0 * depth) ** 0.25
    nblk = seq // MOBA_BLOCK
    nrow = -(-nblk // 16) * 16
    assert past_len // MOBA_BLOCK >= MOBA_TOPK and page_rows <= LANES
    slopes = jnp.asarray([[[_alibi_slope(h, n_heads)] * MOBA_BLOCK] for h in range(n_heads)], F32)

    y_p = x_prompt.reshape(batch * seq, d_model)
    y_s = x_sample.reshape(n_dec, d_model)
    outs = [[] for _ in range(6)]
    for i in range(depth):
        w_in_b = w_in[i].astype(BF16)
        row = lambda a: a[i].reshape(1, -1)
        weights = (w_pool[i].astype(BF16), row(pool_scale), w_branch[i].astype(BF16), w_out[i].astype(BF16),
                   row(ln1_g), row(ln1_b), w_up[i].astype(BF16), w_down[i].astype(BF16),
                   row(ln2_g), row(ln2_b), w_pe[i].astype(BF16), w_pg[i].astype(BF16))

        qs, ks, vs, us, sgs, qg = _proj_sample(y_s, w_in_b, w_in[i])
        ckt = cache_k[i].transpose(0, 2, 3, 1)
        cvt = cache_v[i].transpose(0, 2, 3, 1)

        kt, vtf, u, sg, qt, ka, vt, km = _proj_prompt(y_p, w_in_b, batch, seq)
        km = km.reshape(batch, nblk, n_heads, HEAD_DIM).transpose(0, 2, 1, 3)
        km = jnp.pad(km, ((0, 0), (0, 0), (0, nrow - nblk), (0, HEAD_DIM))).astype(BF16)
        yt = _attn_prompt(slopes * LOG2E, qt, ka, vt, km)
        y_p, sel = _tail_prompt(y_p, yt, u, sg, p_prompt[i].reshape(batch * seq, -1), weights, seq, alpha,
                                page_table, qg.reshape(n_dec, d_attn, 1), ckt)
        outs[0].append(kt.transpose(0, 3, 1, 2))
        outs[1].append(vtf.transpose(0, 3, 1, 2))
        outs[2].append(u.reshape(batch, seq, d_pool)[:, seq - POOL_BUF:])

        q4, k4, v4 = (a.reshape(n_dec, n_heads, 1, HEAD_DIM) for a in (qs, ks, vs))
        sel = sel[:, :, :MOBA_TOPK].reshape(-1)
        ya_s = _attn_sample(page_table, sel, slopes, q4, k4, v4, ckt, cvt, past_len).reshape(n_dec, d_attn)
        state = state_pool[i]
        y_s = _tail_sample(y_s, ya_s, us, state.transpose(1, 0, 2), sgs,
                           p_sample[i].reshape(n_dec, -1), weights, past_len, alpha)
        outs[3].append(ks.reshape(n_dec, 1, n_heads, HEAD_DIM))
        outs[4].append(vs.reshape(n_dec, 1, n_heads, HEAD_DIM))
        outs[5].append(jnp.concatenate([state[:, 1:], us[:, None, :]], axis=1))

    k_p, v_p, pool_p, k_s, v_s, pool_s = (jnp.stack(o) for o in outs)
    return (y_p.reshape(batch, seq, d_model), y_s.reshape(n_dec, 1, d_model), k_p, v_p, pool_p, k_s, v_s, pool_s)
```

```python
import functools

import jax
import jax.numpy as jnp
import numpy as np
from jax import lax
from jax.experimental import pallas as pl
from jax.experimental.pallas import tpu as pltpu

HEAD_DIM = 64
MOBA_BLOCK = 256
MOBA_TOPK = 3
POOL_WINDOWS = (2, 4, 8, 16)
POOL_BUF = max(POOL_WINDOWS) - 1
POOL_HALO = 16
LN_EPS = 1e-5
NEG_INF = -1e30
LANES = 128
VMEM_LIMIT_BYTES = 56 * 1024 * 1024

F32 = jnp.float32
BF16 = jnp.bfloat16


def _dot(a, b):
    return jnp.dot(a, b, preferred_element_type=F32)


def _dot_nt(a, b):
    return lax.dot_general(a, b, (((1,), (1,)), ((), ())), preferred_element_type=F32)


def _resident(block_shape, index_map):
    return pl.BlockSpec(block_shape, index_map, pipeline_mode=pl.Buffered(1))


def _params(semantics):
    return pltpu.CompilerParams(dimension_semantics=semantics, vmem_limit_bytes=VMEM_LIMIT_BYTES)


def _sigmoid(x):
    return 1.0 / (1.0 + jnp.exp(-x))


def _layer_norm(x, g, b):
    mu = jnp.mean(x, axis=-1, keepdims=True)
    xc = x - mu
    var = jnp.mean(xc * xc, axis=-1, keepdims=True)
    return xc * lax.rsqrt(var + LN_EPS) * g + b


LOG2E = 1.4426950408889634
V_PAD = 16


def _alibi_slope(h, n_heads):
    return 2.0 ** (-8.0 * (h + 1) / n_heads)


def _split_bf16(c):
    hi = float(np.asarray(c, dtype=BF16))
    lo = float(np.asarray(c - hi, dtype=BF16))
    return hi, lo


def _proj_prompt_kernel(x_ref, w_ref, k_ref, v_ref, u_ref, sg_ref, qt_ref, ka_ref, vt_ref, km_ref,
                        *, d_attn, d_pool, n_heads):
    tm = x_ref.shape[0]
    xb = x_ref[...].astype(BF16)
    o_k, o_v, o_u, o_g = d_attn, 2 * d_attn, 3 * d_attn, 3 * d_attn + d_pool

    q = _dot(xb, w_ref[:, 0:o_k]) * (LOG2E * HEAD_DIM ** -0.5)
    qt = q.T
    row = lax.broadcasted_iota(jnp.int32, (HEAD_DIM, tm), 0)
    for h in range(n_heads):
        c_hi, c_lo = _split_bf16(LOG2E * _alibi_slope(h, n_heads))
        qt_ref[0, h, 0:HEAD_DIM, :] = qt[h * HEAD_DIM:(h + 1) * HEAD_DIM, :].astype(BF16)
        extra = jnp.where(row == 0, 16.0 * c_hi, jnp.where(row == 1, c_hi,
                          jnp.where(row == 2, 16.0 * c_lo, jnp.where(row == 3, c_lo, 0.0))))
        qt_ref[0, h, HEAD_DIM:2 * HEAD_DIM, :] = extra.astype(BF16)

    k = _dot(xb, w_ref[:, o_k:o_v])
    kt = k.T
    for h in range(n_heads):
        k_ref[0, h] = kt[h * HEAD_DIM:(h + 1) * HEAD_DIM, :]
    for c in range(tm // MOBA_BLOCK):
        blk = k[c * MOBA_BLOCK:(c + 1) * MOBA_BLOCK, :]
        km_ref[c] = jnp.sum(blk, axis=0, keepdims=True) * (1.0 / MOBA_BLOCK)
    prow = lax.broadcasted_iota(jnp.int32, (tm, LANES), 0) & (MOBA_BLOCK - 1)
    lane = lax.broadcasted_iota(jnp.int32, (tm, LANES), 1)
    sixteens = jnp.logical_or(lane == HEAD_DIM, lane == HEAD_DIM + 2)
    units = jnp.logical_or(lane == HEAD_DIM + 1, lane == HEAD_DIM + 3)
    posf = jnp.where(sixteens, (prow >> 4).astype(F32), jnp.where(units, (prow & 15).astype(F32), 0.0))
    low = lane < HEAD_DIM
    for m in range(n_heads // 2):
        col = k[:, m * LANES:(m + 1) * LANES]
        ka_ref[0, 2 * m] = jnp.where(low, col, posf).astype(BF16)
        ka_ref[0, 2 * m + 1] = jnp.where(low, pltpu.roll(col, HEAD_DIM, 1), posf).astype(BF16)

    vt = _dot(xb, w_ref[:, o_v:o_u]).T
    ones_row = jnp.where(lax.broadcasted_iota(jnp.int32, (V_PAD, MOBA_BLOCK), 0) == 0, 1.0, 0.0).astype(BF16)
    for h in range(n_heads):
        v_ref[0, h] = vt[h * HEAD_DIM:(h + 1) * HEAD_DIM, :]
        for c in range(tm // MOBA_BLOCK):
            vt_ref[0, h, c, 0:HEAD_DIM, :] = vt[h * HEAD_DIM:(h + 1) * HEAD_DIM,
                                                c * MOBA_BLOCK:(c + 1) * MOBA_BLOCK].astype(BF16)
            vt_ref[0, h, c, HEAD_DIM:HEAD_DIM + V_PAD, :] = ones_row

    u_ref[...] = _dot(xb, w_ref[:, o_u:o_g])
    sg_ref[...] = _sigmoid(_dot(xb, w_ref[:, o_g:])).astype(BF16)


def _proj_prompt(x2d, w_in, batch, seq, tm=512):
    t, d_model = x2d.shape
    d_attn = d_model // 2
    d_pool = d_model // 2
    n_heads = d_attn // HEAD_DIM
    d_in = w_in.shape[1]
    n_gate = d_in - 3 * d_attn - d_pool
    nt = seq // tm
    nblk = seq // MOBA_BLOCK
    cb = tm // MOBA_BLOCK
    out_shape = (
        jax.ShapeDtypeStruct((batch, n_heads, HEAD_DIM, seq), F32),
        jax.ShapeDtypeStruct((batch, n_heads, HEAD_DIM, seq), F32),
        jax.ShapeDtypeStruct((t, d_pool), F32),
        jax.ShapeDtypeStruct((t, n_gate), BF16),
        jax.ShapeDtypeStruct((batch, n_heads, 2 * HEAD_DIM, seq), BF16),
        jax.ShapeDtypeStruct((batch, n_heads, seq, 2 * HEAD_DIM), BF16),
        jax.ShapeDtypeStruct((batch, n_heads, nblk, HEAD_DIM + V_PAD, MOBA_BLOCK), BF16),
        jax.ShapeDtypeStruct((t // MOBA_BLOCK, 1, d_attn), F32),
    )
    out_specs = (
        pl.BlockSpec((1, n_heads, HEAD_DIM, tm), lambda i: (i // nt, 0, 0, i % nt)),
        pl.BlockSpec((1, n_heads, HEAD_DIM, tm), lambda i: (i // nt, 0, 0, i % nt)),
        pl.BlockSpec((tm, d_pool), lambda i: (i, 0)),
        pl.BlockSpec((tm, n_gate), lambda i: (i, 0)),
        pl.BlockSpec((1, n_heads, 2 * HEAD_DIM, tm), lambda i: (i // nt, 0, 0, i % nt)),
        pl.BlockSpec((1, n_heads, tm, 2 * HEAD_DIM), lambda i: (i // nt, 0, i % nt, 0)),
        pl.BlockSpec((1, n_heads, cb, HEAD_DIM + V_PAD, MOBA_BLOCK), lambda i: (i // nt, 0, i % nt, 0, 0)),
        pl.BlockSpec((cb, 1, d_attn), lambda i: (i, 0, 0)),
    )
    return pl.pallas_call(
        functools.partial(_proj_prompt_kernel, d_attn=d_attn, d_pool=d_pool, n_heads=n_heads),
        out_shape=out_shape,
        grid=(t // tm,),
        in_specs=[pl.BlockSpec((tm, d_model), lambda i: (i, 0)),
                  _resident((d_model, d_in), lambda i: (0, 0))],
        out_specs=out_specs,
        compiler_params=_params(("arbitrary",)),
        name="proj_prompt",
    )(x2d, w_in)


def _attn_prompt_kernel(slope_ref, qt_ref, ka_ref, vt_ref, km_ref, o_ref, mask_ref, m_ref, acc_ref, s_ref):
    qi = pl.program_id(1)
    n_heads, nrow, tq = mask_ref.shape
    own = 2
    row = lax.broadcasted_iota(jnp.int32, (nrow, tq), 0).astype(F32)
    qif = jnp.full((1, tq), qi, jnp.int32).astype(F32)
    ik = lax.broadcasted_iota(jnp.int32, (MOBA_BLOCK, tq), 0)
    iq = lax.broadcasted_iota(jnp.int32, (MOBA_BLOCK, tq), 1)

    def scores(j, slot, heads=range(n_heads)):
        off = pl.multiple_of(j * MOBA_BLOCK, MOBA_BLOCK)
        for h in heads:
            s_ref[slot, h] = _dot(ka_ref[0, h, pl.ds(off, MOBA_BLOCK), :], qt_ref[0, h])

    def absorb(j, slot, heads=range(n_heads)):
        for h in heads:
            ps, alphas = [], []
            for c in range(tq // LANES):
                cols = slice(c * LANES, (c + 1) * LANES)
                s = s_ref[slot, h, :, cols]
                bias = mask_ref[h, pl.ds(j, 1), :][:, cols]
                m = m_ref[h, :, cols]
                m_new = jnp.maximum(m, jnp.max(s, axis=0, keepdims=True) + bias)
                alphas.append(jnp.exp2(m - m_new))
                ps.append(jnp.exp2(s - (m_new - bias)).astype(BF16))
                m_ref[h, :, cols] = m_new
            p = jnp.concatenate(ps, axis=1)
            acc_ref[h] = jnp.concatenate(alphas, axis=1) * acc_ref[h] + _dot(vt_ref[0, h, j], p)

    gates = [_dot(km_ref[0, h], qt_ref[0, h]) for h in range(n_heads)]
    for h in range(n_heads):
        scores(qi, own, [h])
        g = jnp.where(row < qif, gates[h], NEG_INF)
        sel = jnp.zeros((nrow, tq), jnp.bool_)
        for r in range(MOBA_TOPK):
            mx = jnp.max(g, axis=0, keepdims=True)
            idx = jnp.min(jnp.where(g == mx, row, float(nrow)), axis=0, keepdims=True)
            hit = row == idx
            sel = jnp.logical_or(sel, jnp.logical_and(hit, qif > float(r)))
            g = jnp.where(hit, -jnp.inf, g)
        mask_ref[h] = jnp.where(sel, slope_ref[h] * float(MOBA_BLOCK) * (row - qif), NEG_INF)

    for h in range(n_heads):
        scores(0, 0, [h])
        s = jnp.where(ik <= iq, s_ref[own, h], NEG_INF)
        m0 = jnp.max(s, axis=0, keepdims=True)
        p = jnp.exp2(s - m0)
        m_ref[h] = m0
        acc_ref[h] = _dot(vt_ref[0, h, qi], p.astype(BF16))

    for h in range(n_heads):
        scores(1, 1, [h])
        absorb(0, 0, [h])

    def body(jj, carry):
        j = 2 * jj
        for h in range(n_heads):
            scores(j, 0, [h])
            absorb(j - 1, 1, [h])
        for h in range(n_heads):
            scores(j + 1, 1, [h])
            absorb(j, 0, [h])
        return carry

    pairs = jnp.maximum((qi + 1) >> 1, 1)
    lax.fori_loop(1, pairs, body, 0)
    last = 2 * pairs - 1
    for h in range(n_heads):
        absorb(last, 1, [h])
        o_ref[0, h] = acc_ref[h, 0:HEAD_DIM, :] * (1.0 / acc_ref[h, HEAD_DIM:HEAD_DIM + 1, :])


def _attn_prompt(slopes, qt, ka, vt, km):
    batch, n_heads, _, seq = qt.shape
    nblk = seq // MOBA_BLOCK
    nrow = km.shape[2]
    per_batch = lambda nd: (lambda b, i: (b,) + (0,) * (nd - 1))
    return pl.pallas_call(
        _attn_prompt_kernel,
        out_shape=jax.ShapeDtypeStruct((batch, n_heads, HEAD_DIM, seq), F32),
        grid=(batch, nblk),
        in_specs=[
            pl.BlockSpec(slopes.shape, lambda b, i: (0, 0, 0)),
            pl.BlockSpec((1, n_heads, 2 * HEAD_DIM, MOBA_BLOCK), lambda b, i: (b, 0, 0, i)),
            _resident((1,) + ka.shape[1:], per_batch(4)),
            _resident((1,) + vt.shape[1:], per_batch(5)),
            pl.BlockSpec((1,) + km.shape[1:], per_batch(4)),
        ],
        out_specs=pl.BlockSpec((1, n_heads, HEAD_DIM, MOBA_BLOCK), lambda b, i: (b, 0, 0, i)),
        scratch_shapes=[pltpu.VMEM((n_heads, nrow, MOBA_BLOCK), F32),
                        pltpu.VMEM((n_heads, 1, MOBA_BLOCK), F32),
                        pltpu.VMEM((n_heads, HEAD_DIM + V_PAD, MOBA_BLOCK), F32),
                        pltpu.VMEM((3, n_heads, MOBA_BLOCK, MOBA_BLOCK), F32)],
        compiler_params=_params(("arbitrary", "arbitrary")),
        name="attn_prompt",
    )(slopes, qt, ka, vt, km)


def _tail(x, ya_b, pool_d, sg_ref, p_ref, wp_ref, ps_ref, wb_ref, wo_ref, g1_ref, b1_ref, wu_ref, wd_ref,
          g2_ref, b2_ref, wpe_ref, wpg_ref, o_ref, yb_ref, *, alpha, ff_chunk, side_work=(), after_mlp=None):
    d_model = x.shape[1]
    pool_ch = wp_ref.shape[1]
    br0 = _dot(ya_b, wb_ref[0])
    d = pool_d()
    for g in range(len(POOL_WINDOWS)):
        sl = slice(g * pool_ch, (g + 1) * pool_ch)
        yb = _dot(d[g].astype(BF16), wp_ref[g]) * ps_ref[:, sl]
        yb_ref[:, sl] = yb.astype(BF16)
    br1 = _dot(yb_ref[...], wb_ref[1])
    merged = sg_ref[:, 0:d_model].astype(F32) * br0 + sg_ref[:, d_model:].astype(F32) * br1
    mix = _dot(merged.astype(BF16), wo_ref[...])
    pe = _dot(p_ref[...].astype(BF16), wpe_ref[...])
    h1 = _layer_norm(alpha * x + mix, g1_ref[...], b1_ref[...])
    h1b = h1.astype(BF16)
    ple = _sigmoid(_dot(h1b, wpg_ref[...])) * pe
    acc = alpha * h1 + ple
    d_ff = wu_ref.shape[1]
    assert len(side_work) <= d_ff // ff_chunk
    for c in range(d_ff // ff_chunk):
        a = jnp.maximum(_dot(h1b, wu_ref[:, c * ff_chunk:(c + 1) * ff_chunk]), 0.0)
        if c < len(side_work):
            side_work[c]()
        acc = acc + _dot((a * a).astype(BF16), wd_ref[c * ff_chunk:(c + 1) * ff_chunk, :])
    if after_mlp is not None:
        after_mlp()
    o_ref[...] = _layer_norm(acc, g2_ref[...], b2_ref[...])


def _rank_sample_jobs(pages, n_pages, q_ref, g_ref, idx_ref, chunk, pages_per_block, nblk, n_jobs):
    blocks = n_pages // pages_per_block
    assert blocks % n_jobs == 0
    per_job = blocks // n_jobs

    def job(k):
        first = pages(k * per_job * pages_per_block)
        n_heads, head_dim, page_rows = first.shape
        lane = lax.broadcasted_iota(jnp.int32, (n_heads, LANES), 1)
        q = q_ref[0].reshape(n_heads, head_dim, 1)
        g = g_ref[...]
        if k == 0:
            g = jnp.where(chunk == 0, -jnp.inf, g)
        for r in range(k * per_job, (k + 1) * per_job):
            ksum = first if r == k * per_job else pages(r * pages_per_block)
            for s in range(1, pages_per_block):
                ksum = ksum + pages(r * pages_per_block + s)
            kmean = jnp.sum(ksum, axis=2, keepdims=True) * (1.0 / (page_rows * pages_per_block))
            gate = jnp.sum(kmean * q, axis=1)
            g = jnp.where(lane == chunk * blocks + r, gate, g)
        g_ref[...] = g
        if k < n_jobs - 1:
            return
        col = lane.astype(F32)
        out = jnp.zeros((n_heads, LANES), F32)
        for r in range(MOBA_TOPK):
            mx = jnp.max(g, axis=1, keepdims=True)
            idx = jnp.min(jnp.where(g == mx, col, float(LANES)), axis=1, keepdims=True)
            out = jnp.where(lane == r, idx, out)
            g = jnp.where(col == idx, -jnp.inf, g)
        idx_ref[0] = jnp.minimum(out.astype(jnp.int32), nblk - 1)

    return [functools.partial(job, k) for k in range(n_jobs)]


def _tail_prompt_kernel(pt_ref, x_ref, yt_ref, u_ref, halo_ref, sg_ref, p_ref, q_ref, ck_ref, *rest,
                        tiles_per_seq, chunks_per_sample, pages_per_block, nblk_sample, alpha, ff_chunk):
    *w_refs, o_ref, idx_ref, yb_ref, ue_ref, g_ref, page_buf, sem = rest
    i = pl.program_id(0)
    slot = i & 1
    n_pages = page_buf.shape[1]

    def page_copy(step, slot_, r):
        return pltpu.make_async_copy(ck_ref.at[pt_ref[step, r]], page_buf.at[slot_, r], sem.at[slot_])

    @pl.when(i == 0)
    def _():
        for r in range(n_pages):
            page_copy(0, 0, r).start()

    def fetch_next():
        @pl.when(i + 1 < pl.num_programs(0))
        def _():
            for r in range(n_pages):
                page_copy(i + 1, 1 - slot, r).start()

    blocks = n_pages // pages_per_block
    n_mlp_chunks = w_refs[6].shape[1] // ff_chunk
    n_jobs = max(j for j in range(1, n_mlp_chunks + 1) if blocks % j == 0)
    jobs = _rank_sample_jobs(lambda r: page_buf[slot, r], n_pages, q_ref, g_ref, idx_ref, i % chunks_per_sample,
                             pages_per_block, nblk_sample, n_jobs)

    def first_job():
        for r in range(n_pages):
            page_copy(i, slot, r).wait()
        jobs[0]()

    side_work = [first_job] + jobs[1:]

    tm = x_ref.shape[0]
    ti = pl.program_id(0) % tiles_per_seq

    def pool_d():
        u = u_ref[...]
        ue_ref[POOL_HALO:POOL_HALO + tm, :] = u
        ue_ref[0:POOL_HALO, :] = jnp.where(ti == 0, 0.0, halo_ref[...])
        pool_ch = u.shape[1] // len(POOL_WINDOWS)
        pos = ti * tm + lax.broadcasted_iota(jnp.int32, (tm, pool_ch), 0)
        d = []
        for g, w in enumerate(POOL_WINDOWS):
            sl = slice(g * pool_ch, (g + 1) * pool_ch)
            pooled = u[:, sl]
            for s in range(1, w):
                pooled = pooled + ue_ref[POOL_HALO - s:POOL_HALO - s + tm, sl]
            cnt = jnp.minimum(w, pos + 1).astype(F32)
            d.append(pooled / cnt - u[:, sl])
        return d

    ya_b = yt_ref[0].T.astype(BF16)
    _tail(x_ref[...], ya_b, pool_d, sg_ref, p_ref, *w_refs, o_ref, yb_ref, alpha=alpha, ff_chunk=ff_chunk,
          side_work=side_work, after_mlp=fetch_next)


def _tail_sample_kernel(x_ref, ya_ref, u_ref, st_ref, sg_ref, p_ref, *rest, cnts, alpha, ff_chunk):
    *w_refs, o_ref, yb_ref = rest
    def pool_d():
        u = u_ref[...]
        pool_ch = u.shape[1] // len(POOL_WINDOWS)
        d = []
        for g, w in enumerate(POOL_WINDOWS):
            pooled = u[:, g * pool_ch:(g + 1) * pool_ch]
            for s in range(1, w):
                pooled = pooled + st_ref[POOL_BUF - s][:, g * pool_ch:(g + 1) * pool_ch]
            d.append(pooled / cnts[g] - u[:, g * pool_ch:(g + 1) * pool_ch])
        return d

    _tail(x_ref[...], ya_ref[...].astype(BF16), pool_d, sg_ref, p_ref, *w_refs, o_ref, yb_ref,
          alpha=alpha, ff_chunk=ff_chunk)


def _weight_specs(weights, index_map_factory):
    return [_resident(w.shape, index_map_factory(w.ndim)) for w in weights]


def _tail_prompt(x2d, yt, u, sg, p2d, weights, seq, alpha, page_table, qcol, cache_kt, tm=256, ff_chunk=1024):
    t, d_model = x2d.shape
    d_pool = u.shape[1]
    batch = t // seq
    nt = seq // tm
    hb = tm // POOL_HALO
    steps = t // tm
    n_dec, n_pages = page_table.shape
    n_heads, head_dim, page_rows = cache_kt.shape[1:]
    ppb = MOBA_BLOCK // page_rows
    nblk = n_pages // ppb
    assert steps % n_dec == 0 and MOBA_TOPK <= nblk <= LANES
    cps = steps // n_dec
    assert n_pages % (cps * ppb) == 0
    pps = n_pages // cps
    yt3 = yt.reshape(batch, yt.shape[1] * yt.shape[2], seq)
    pages_by_step = page_table.reshape(steps, pps)
    in_specs = [
        pl.BlockSpec((tm, d_model), lambda i, pt: (i, 0)),
        pl.BlockSpec((1, yt3.shape[1], tm), lambda i, pt: (i // nt, 0, i % nt)),
        pl.BlockSpec((tm, d_pool), lambda i, pt: (i, 0)),
        pl.BlockSpec((POOL_HALO, d_pool), lambda i, pt: (jnp.maximum(i * hb - 1, 0), 0)),
        pl.BlockSpec((tm, sg.shape[1]), lambda i, pt: (i, 0)),
        pl.BlockSpec((tm, p2d.shape[1]), lambda i, pt: (i, 0)),
        pl.BlockSpec((1, n_heads * head_dim, 1), lambda i, pt: (i // cps, 0, 0)),
        pl.BlockSpec(memory_space=pl.ANY),
    ] + _weight_specs(weights, lambda nd: (lambda i, pt: (0,) * nd))
    grid_spec = pltpu.PrefetchScalarGridSpec(
        num_scalar_prefetch=1,
        grid=(steps,),
        in_specs=in_specs,
        out_specs=(pl.BlockSpec((tm, d_model), lambda i, pt: (i, 0)),
                   pl.BlockSpec((1, n_heads, LANES), lambda i, pt: (i // cps, 0, 0))),
        scratch_shapes=[pltpu.VMEM((tm, d_pool), BF16), pltpu.VMEM((POOL_HALO + tm, d_pool), F32),
                        pltpu.VMEM((n_heads, LANES), F32),
                        pltpu.VMEM((2, pps, n_heads, head_dim, page_rows), F32),
                        pltpu.SemaphoreType.DMA((2,))],
    )
    return pl.pallas_call(
        functools.partial(_tail_prompt_kernel, tiles_per_seq=nt, chunks_per_sample=cps,
                          pages_per_block=ppb, nblk_sample=nblk, alpha=alpha, ff_chunk=ff_chunk),
        out_shape=(jax.ShapeDtypeStruct((t, d_model), F32),
                   jax.ShapeDtypeStruct((n_dec, n_heads, LANES), jnp.int32)),
        grid_spec=grid_spec,
        compiler_params=_params(("arbitrary",)),
        name="tail_prompt",
    )(pages_by_step, x2d, yt3, u, u, sg, p2d, qcol, cache_kt, *weights)


def _tail_sample(x2d, ya, u, state_t, sg, p2d, weights, past_len, alpha, ff_chunk=1024):
    n, d_model = x2d.shape
    d_pool = u.shape[1]
    cnts = tuple(float(min(w, past_len + 1)) for w in POOL_WINDOWS)
    operands = (x2d, ya, u, state_t, sg, p2d)
    whole = lambda nd: (lambda i: (0,) * nd)
    in_specs = [pl.BlockSpec(a.shape, whole(a.ndim)) for a in operands] + _weight_specs(weights, whole)
    return pl.pallas_call(
        functools.partial(_tail_sample_kernel, cnts=cnts, alpha=alpha, ff_chunk=ff_chunk),
        out_shape=jax.ShapeDtypeStruct((n, d_model), F32),
        grid=(1,),
        in_specs=in_specs,
        out_specs=pl.BlockSpec((n, d_model), lambda i: (0, 0)),
        scratch_shapes=[pltpu.VMEM((n, d_pool), BF16)],
        compiler_params=_params(("arbitrary",)),
        name="tail_sample",
    )(*operands, *weights)


def _proj_sample_kernel(x_ref, w_ref, wq_ref, q_ref, k_ref, v_ref, u_ref, sg_ref, qg_ref, *, d_attn, d_pool):
    x = x_ref[...]
    z = _dot(x.astype(BF16), w_ref[...])
    q_ref[...] = z[:, 0:d_attn] * (HEAD_DIM ** -0.5)
    k_ref[...] = z[:, d_attn:2 * d_attn]
    v_ref[...] = z[:, 2 * d_attn:3 * d_attn]
    u_ref[...] = z[:, 3 * d_attn:3 * d_attn + d_pool]
    sg_ref[...] = _sigmoid(z[:, 3 * d_attn + d_pool:]).astype(BF16)
    qg_ref[...] = jnp.dot(x, wq_ref[...], precision=lax.Precision.HIGHEST, preferred_element_type=F32)


def _proj_sample(x2d, w_in, w_in_f32):
    n, d_model = x2d.shape
    d_attn = d_model // 2
    d_pool = d_model // 2
    n_gate = w_in.shape[1] - 3 * d_attn - d_pool
    shapes = [(n, d_attn)] * 3 + [(n, d_pool)]
    whole = lambda s: pl.BlockSpec(s, lambda i: (0, 0))
    return pl.pallas_call(
        functools.partial(_proj_sample_kernel, d_attn=d_attn, d_pool=d_pool),
        out_shape=tuple(jax.ShapeDtypeStruct(s, F32) for s in shapes)
        + (jax.ShapeDtypeStruct((n, n_gate), BF16), jax.ShapeDtypeStruct((n, d_attn), F32)),
        grid=(1,),
        in_specs=[whole(x2d.shape), whole(w_in.shape), whole((d_model, d_attn))],
        out_specs=tuple(whole(s) for s in shapes) + (whole((n, n_gate)), whole((n, d_attn))),
        compiler_params=_params(("arbitrary",)),
        name="proj_sample",
    )(x2d, w_in, w_in_f32)


def _attn_sample_kernel(pt_ref, sel_ref, slope_ref, q_ref, kn_ref, vn_ref, ck_ref, cv_ref, o_ref,
                        kbuf, vbuf, sem, *, pages_per_block, past_len):
    b = pl.program_id(0)
    n_seq = pl.num_programs(0)
    n_heads = q_ref.shape[1]
    head_dim, page_rows = kbuf.shape[2:]
    tiles_per_head = MOBA_TOPK * pages_per_block
    slot = b & 1

    def tile_copies(seq, slot_, h, t):
        r, s = divmod(t, pages_per_block)
        blk = sel_ref[(seq * n_heads + h) * MOBA_TOPK + r]
        page = pt_ref[seq, blk * pages_per_block + s]
        i = h * tiles_per_head + t
        return (pltpu.make_async_copy(ck_ref.at[page, h], kbuf.at[slot_, i], sem.at[slot_]),
                pltpu.make_async_copy(cv_ref.at[page, h], vbuf.at[slot_, i], sem.at[slot_]))

    def start_gather(seq, slot_):
        for h in range(n_heads):
            for t in range(tiles_per_head):
                for copy in tile_copies(seq, slot_, h, t):
                    copy.start(priority=(h * tiles_per_head + t) % 2)

    @pl.when(b == 0)
    def _():
        start_gather(0, 0)

    @pl.when(b + 1 < n_seq)
    def _():
        start_gather(b + 1, 1 - slot)

    for h in range(n_heads):
        for t in range(tiles_per_head):
            for copy in tile_copies(b, slot, h, t):
                copy.wait()

    kpos = lax.broadcasted_iota(jnp.int32, (1, page_rows), 1)
    for h in range(n_heads):
        q = q_ref[0, h].astype(BF16)
        q8 = jnp.broadcast_to(q, (8, head_dim))
        slope = slope_ref[h][:, 0:page_rows]
        s_self = jnp.sum(q.astype(F32) * kn_ref[0, h].astype(BF16).astype(F32), axis=1, keepdims=True)
        scores = []
        m = s_self
        for t in range(tiles_per_head):
            r, s = divmod(t, pages_per_block)
            blk = sel_ref[(b * n_heads + h) * MOBA_TOPK + r]
            kt = kbuf[slot, h * tiles_per_head + t].astype(BF16)
            dist = past_len - (blk * MOBA_BLOCK + s * page_rows + kpos)
            sc = _dot(q8, kt)[0:1, :] - slope * dist.astype(F32)
            scores.append(sc)
            m = jnp.maximum(m, jnp.max(sc, axis=1, keepdims=True))
        p_self = jnp.exp(s_self - m)
        l = p_self
        acc = p_self * vn_ref[0, h]
        for t, sc in enumerate(scores):
            p = jnp.exp(sc - m)
            l = l + jnp.sum(p, axis=1, keepdims=True)
            p8 = jnp.broadcast_to(p.astype(BF16), (8, page_rows))
            acc = acc + _dot_nt(p8, vbuf[slot, h * tiles_per_head + t].astype(BF16))[0:1, :]
        o_ref[0, h] = acc / l


def _attn_sample(page_table, sel, slopes, q4, k4, v4, cache_kt, cache_vt, past_len):
    n, n_heads, _, head_dim = q4.shape
    page_rows = cache_kt.shape[3]
    ppb = MOBA_BLOCK // page_rows
    n_tiles = n_heads * MOBA_TOPK * ppb
    tok = pl.BlockSpec((1, n_heads, 1, head_dim), lambda b, pt, sl: (b, 0, 0, 0))
    grid_spec = pltpu.PrefetchScalarGridSpec(
        num_scalar_prefetch=2,
        grid=(n,),
        in_specs=[pl.BlockSpec(slopes.shape, lambda b, pt, sl: (0, 0, 0)), tok, tok, tok,
                  pl.BlockSpec(memory_space=pl.ANY), pl.BlockSpec(memory_space=pl.ANY)],
        out_specs=tok,
        scratch_shapes=[pltpu.VMEM((2, n_tiles, head_dim, page_rows), F32),
                        pltpu.VMEM((2, n_tiles, head_dim, page_rows), F32),
                        pltpu.SemaphoreType.DMA((2,))],
    )
    return pl.pallas_call(
        functools.partial(_attn_sample_kernel, pages_per_block=ppb, past_len=past_len),
        out_shape=jax.ShapeDtypeStruct(q4.shape, F32),
        grid_spec=grid_spec,
        compiler_params=_params(("arbitrary",)),
        name="attn_sample",
    )(page_table, sel, slopes, q4, k4, v4, cache_kt, cache_vt)


def kernel(x_prompt, x_sample, cache_k, cache_v, state_pool, page_table, p_prompt, p_sample, w_in, w_branch,
           w_out, ln1_g, ln1_b, w_up, w_down, ln2_g, ln2_b, w_pe, w_pg, w_pool, pool_scale):
    depth = w_in.shape[0]
    batch, seq, d_model = x_prompt.shape
    n_dec, dec_seq, _ = x_sample.shape
    assert dec_seq == 1, "the sample path handles one new token per sequence"
    d_attn = d_model // 2
    d_pool = d_model // 2
    n_heads = d_attn // HEAD_DIM
    n_phys, page_rows = cache_k.shape[1:3]
    past_len = page_table.shape[1] * page_rows
    alpha = (2.0 * depth) ** 0.25
    nblk = seq // MOBA_BLOCK
    nrow = -(-nblk // 16) * 16
    assert past_len // MOBA_BLOCK >= MOBA_TOPK and page_rows <= LANES
    slopes = jnp.asarray([[[_alibi_slope(h, n_heads)] * MOBA_BLOCK] for h in range(n_heads)], F32)

    y_p = x_prompt.reshape(batch * seq, d_model)
    y_s = x_sample.reshape(n_dec, d_model)
    outs = [[] for _ in range(6)]
    for i in range(depth):
        w_in_b = w_in[i].astype(BF16)
        row = lambda a: a[i].reshape(1, -1)
        weights = (w_pool[i].astype(BF16), row(pool_scale), w_branch[i].astype(BF16), w_out[i].astype(BF16),
                   row(ln1_g), row(ln1_b), w_up[i].astype(BF16), w_down[i].astype(BF16),
                   row(ln2_g), row(ln2_b), w_pe[i].astype(BF16), w_pg[i].astype(BF16))

        qs, ks, vs, us, sgs, qg = _proj_sample(y_s, w_in_b, w_in[i])
        ckt = cache_k[i].transpose(0, 2, 3, 1)
        cvt = cache_v[i].transpose(0, 2, 3, 1)

        kt, vtf, u, sg, qt, ka, vt, km = _proj_prompt(y_p, w_in_b, batch, seq)
        km = km.reshape(batch, nblk, n_heads, HEAD_DIM).transpose(0, 2, 1, 3)
        km = jnp.pad(km, ((0, 0), (0, 0), (0, nrow - nblk), (0, HEAD_DIM))).astype(BF16)
        yt = _attn_prompt(slopes * LOG2E, qt, ka, vt, km)
        y_p, sel = _tail_prompt(y_p, yt, u, sg, p_prompt[i].reshape(batch * seq, -1), weights, seq, alpha,
                                page_table, qg.reshape(n_dec, d_attn, 1), ckt)
        outs[0].append(kt.transpose(0, 3, 1, 2))
        outs[1].append(vtf.transpose(0, 3, 1, 2))
        outs[2].append(u.reshape(batch, seq, d_pool)[:, seq - POOL_BUF:])

        q4, k4, v4 = (a.reshape(n_dec, n_heads, 1, HEAD_DIM) for a in (qs, ks, vs))
        sel = sel[:, :, :MOBA_TOPK].reshape(-1)
        ya_s = _attn_sample(page_table, sel, slopes, q4, k4, v4, ckt, cvt, past_len).reshape(n_dec, d_attn)
        state = state_pool[i]
        y_s = _tail_sample(y_s, ya_s, us, state.transpose(1, 0, 2), sgs,
                           p_sample[i].reshape(n_dec, -1), weights, past_len, alpha)
        outs[3].append(ks.reshape(n_dec, 1, n_heads, HEAD_DIM))
        outs[4].append(vs.reshape(n_dec, 1, n_heads, HEAD_DIM))
        outs[5].append(jnp.concatenate([state[:, 1:], us[:, None, :]], axis=1))

    k_p, v_p, pool_p, k_s, v_s, pool_s = (jnp.stack(o) for o in outs)
    return (y_p.reshape(batch, seq, d_model), y_s.reshape(n_dec, 1, d_model), k_p, v_p, pool_p, k_s, v_s, pool_s)
```
